```python
import math
import jax, jax.numpy as jnp
from jax import lax
import numpy as np

D_MODEL = 2048
BATCH = 8
SEQ = 2048
DEPTH = 1
DEC_BATCH = 1
DEC_SEQ = 16384
PAST_LEN = 128

GDN_HEADS = 8
GDN_DK = 128
GDN_DV = 128
GDN_QK = GDN_HEADS * GDN_DK
GDN_WIDTH = GDN_HEADS * GDN_DV
GDN_QKV = 2 * GDN_QK + GDN_WIDTH
GDN_CONV = 5
GDN_CHUNK = 64
DIFF_HEADS = 4
DIFF_HD = 128
DIFF_QK = DIFF_HEADS * 2 * DIFF_HD
DIFF_WIDTH = DIFF_HEADS * 2 * DIFF_HD
ATTN_QBLOCK = 128
PEER_HEADS = 8
PEER_NKEYS = 128
PEER_EXPERTS = PEER_NKEYS * PEER_NKEYS
PEER_DQ = 256
PEER_TOPK = 16
PEER_TBLOCK = 128
NORM_EPS = 1e-6

IN_SIZES = (GDN_QKV, GDN_WIDTH, 2 * GDN_HEADS, 2 * GDN_HEADS, DIFF_QK, DIFF_QK, DIFF_WIDTH, 2 * D_MODEL)
IN_COLS = GDN_QKV + GDN_WIDTH + 4 * GDN_HEADS + 2 * DIFF_QK + DIFF_WIDTH + 2 * D_MODEL

kernel_name = 'hybrid_gdn_diffattn_peer_encoder'


def rms_norm(x, gain):
    xf = x.astype(jnp.float32)
    y = xf * lax.rsqrt(jnp.mean(xf * xf, axis=-1, keepdims=True) + NORM_EPS)
    return (y * gain.astype(jnp.float32)).astype(x.dtype)


def l2_normalize(x):
    return x * lax.rsqrt(jnp.sum(x * x, axis=-1, keepdims=True) + 1e-6)


def split_cols(t):
    idx, acc = [], 0
    for s in IN_SIZES[:-1]:
        acc += s
        idx.append(acc)
    return jnp.split(t, idx, axis=-1)


def centred_depthwise_conv(x, w):
    pad = (GDN_CONV - 1) // 2
    return lax.conv_general_dilated(
        x, w[:, None, :].astype(x.dtype), window_strides=(1,), padding=[(pad, pad)],
        dimension_numbers=('NWC', 'WIO', 'NWC'), feature_group_count=x.shape[-1])


def gated_delta_chunked(q, k, v, g, beta):
    B, L, H, dk = q.shape
    dv = v.shape[-1]
    C = GDN_CHUNK
    N = L // C

    def chunks(t):
        t = jnp.moveaxis(t, 2, 1)
        return t.reshape((B, H, N, C) + t.shape[3:])

    q, k, v, g, beta = (chunks(t) for t in (q, k, v, g, beta))
    gc = jnp.cumsum(g, axis=-1)
    tril = jnp.tril(jnp.ones((C, C), bool))
    strict = jnp.tril(jnp.ones((C, C), bool), -1)
    diff = gc[..., :, None] - gc[..., None, :]
    decay = jnp.where(tril, jnp.exp(jnp.where(tril, diff, 0.0)), 0.0)
    kb = k * beta[..., None]
    a = jnp.where(strict, jnp.einsum('bhnid,bhnjd->bhnij', kb, k) * decay, 0.0)
    eye = jnp.eye(C, dtype=q.dtype)
    t_inv = lax.linalg.triangular_solve(eye + a, jnp.broadcast_to(eye, a.shape),
                                        left_side=True, lower=True, unit_diagonal=True)
    u = jnp.einsum('bhnij,bhnje->bhnie', t_inv, v * beta[..., None])
    w = jnp.einsum('bhnij,bhnjd->bhnid', t_inv, kb * jnp.exp(gc)[..., None])
    qk = jnp.where(tril, jnp.einsum('bhnid,bhnjd->bhnij', q, k) * decay, 0.0)
    q_dec = q * jnp.exp(gc)[..., None]
    g_last = gc[..., -1]
    k_tail = k * jnp.exp(g_last[..., None] - gc)[..., None]

    def step(state, inp):
        q_i, qk_i, w_i, u_i, kt_i, gl_i = inp
        v_new = u_i - jnp.einsum('bhcd,bhde->bhce', w_i, state)
        o_i = jnp.einsum('bhcd,bhde->bhce', q_i, state) + jnp.einsum('bhij,bhje->bhie', qk_i, v_new)
        state = state * jnp.exp(gl_i)[..., None, None] + jnp.einsum('bhcd,bhce->bhde', kt_i, v_new)
        return state, o_i

    xs = tuple(jnp.moveaxis(t, 2, 0) for t in (q_dec, qk, w, u, k_tail, g_last))
    s0 = jnp.zeros((B, H, dk, dv), q.dtype)
    _, o = lax.scan(step, s0, xs)
    o = jnp.moveaxis(o, 0, 2).reshape(B, H, L, dv)
    return jnp.moveaxis(o, 1, 2)


def differential_attention(q, k, v, lam):
    B, S, H, _, d = q.shape
    nq = S // ATTN_QBLOCK
    slopes = 2.0 ** (-8.0 * jnp.arange(1, H + 1, dtype=jnp.float32) / H)
    kpos = jnp.arange(S, dtype=jnp.int32)
    qblocks = jnp.moveaxis(q.reshape(B, nq, ATTN_QBLOCK, H, 2, d), 1, 0)
    starts = jnp.arange(nq, dtype=jnp.int32) * ATTN_QBLOCK
    scale = d ** -0.5

    def one_block(args):
        qblk, start = args
        s = jnp.einsum('bqhcd,bkhcd->bhcqk', qblk, k).astype(jnp.float32) * scale
        qpos = start + jnp.arange(ATTN_QBLOCK, dtype=jnp.int32)
        dist = jnp.abs(qpos[:, None] - kpos[None, :]).astype(jnp.float32)
        s = s - slopes[None, :, None, None, None] * dist
        p = jax.nn.softmax(s, axis=-1)
        amap = p[:, :, 0] - lam * p[:, :, 1]
        return jnp.einsum('bhqk,bkhe->bqhe', amap.astype(v.dtype), v)

    o = lax.map(one_block, (qblocks, starts))
    return jnp.moveaxis(o, 0, 1).reshape(B, S, H, v.shape[-1])


def peer_ffn(h, w_q, sub_keys, u, v):
    B, S, D = h.shape
    T = B * S
    ht = h.reshape(T, D)
    q = (ht @ w_q).reshape(T, PEER_HEADS, 2, PEER_DQ // 2)
    s = jnp.einsum('thcd,hcnd->thcn', q, sub_keys).astype(jnp.float32)
    sv, si = lax.top_k(s, PEER_TOPK)
    cand_s = (sv[:, :, 0, :, None] + sv[:, :, 1, None, :]).reshape(T, PEER_HEADS, PEER_TOPK * PEER_TOPK)
    cand_i = (si[:, :, 0, :, None] * PEER_NKEYS + si[:, :, 1, None, :]).reshape(T, PEER_HEADS, PEER_TOPK * PEER_TOPK)
    top_s, top_pos = lax.top_k(cand_s, PEER_TOPK)
    idx = jnp.take_along_axis(cand_i, top_pos, axis=-1)
    gate = jax.nn.softmax(top_s, axis=-1)
    nb = T // PEER_TBLOCK
    xs = (ht.reshape(nb, PEER_TBLOCK, D),
          idx.reshape(nb, PEER_TBLOCK, PEER_HEADS * PEER_TOPK),
          gate.reshape(nb, PEER_TBLOCK, PEER_HEADS * PEER_TOPK))

    def one_block(args):
        xb, ib, gb = args
        ue = jnp.take(u, ib, axis=0)
        act = jax.nn.gelu(jnp.einsum('tkd,td->tk', ue, xb).astype(jnp.float32))
        coef = (gb * act).astype(v.dtype)
        ve = jnp.take(v, ib, axis=0)
        return jnp.einsum('tk,tkd->td', coef, ve)

    y = lax.map(one_block, xs)
    return y.reshape(B, S, D).astype(h.dtype)


def encoder_trunk(x, norm_mix_gain, w_in, conv_w, gdn_a_log, gdn_dt_bias, gdn_norm_gain,
                  diff_lambda, diff_norm_gain, w_branch_a, w_branch_b, w_out, norm_ffn_gain,
                  peer_w_q, peer_sub_keys, peer_u, peer_v, norm_final_gain):
    B, S, _ = x.shape
    f32 = jnp.float32
    flip = lambda t: jnp.flip(t, axis=1)
    for l in range(DEPTH):
        h = rms_norm(x, norm_mix_gain[l])
        qkv_a, z_a, a_in, b_in, q_b, k_b, v_b, gates = split_cols(h @ w_in[l])

        qkv_a = jax.nn.silu(centred_depthwise_conv(qkv_a, conv_w[l]))
        qa, ka, va = jnp.split(qkv_a, [GDN_QK, 2 * GDN_QK], axis=-1)
        qa = l2_normalize(qa.reshape(B, S, GDN_HEADS, GDN_DK).astype(f32)) * (GDN_DK ** -0.5)
        ka = l2_normalize(ka.reshape(B, S, GDN_HEADS, GDN_DK).astype(f32))
        va = va.reshape(B, S, GDN_HEADS, GDN_DV).astype(f32)
        g = -jnp.exp(gdn_a_log[l].astype(f32)) * jax.nn.softplus(
            a_in.reshape(B, S, 2, GDN_HEADS).astype(f32) + gdn_dt_bias[l].astype(f32))
        beta = jax.nn.sigmoid(b_in.reshape(B, S, 2, GDN_HEADS).astype(f32))
        o_fwd = gated_delta_chunked(qa, ka, va, g[:, :, 0], beta[:, :, 0])
        o_bwd = flip(gated_delta_chunked(flip(qa), flip(ka), flip(va), flip(g[:, :, 1]), flip(beta[:, :, 1])))
        o_a = rms_norm(o_fwd + o_bwd, gdn_norm_gain[l]) * jax.nn.silu(
            z_a.reshape(B, S, GDN_HEADS, GDN_DV).astype(f32))
        o_a = o_a.reshape(B, S, GDN_WIDTH).astype(x.dtype)

        lam_p = diff_lambda[l].astype(f32)
        lam_init = 0.8 - 0.6 * math.exp(-0.3 * l)
        lam = jnp.exp(jnp.sum(lam_p[0] * lam_p[1])) - jnp.exp(jnp.sum(lam_p[2] * lam_p[3])) + lam_init
        o_b = differential_attention(q_b.reshape(B, S, DIFF_HEADS, 2, DIFF_HD),
                                     k_b.reshape(B, S, DIFF_HEADS, 2, DIFF_HD),
                                     v_b.reshape(B, S, DIFF_HEADS, 2 * DIFF_HD), lam)
        o_b = (rms_norm(o_b.astype(f32), diff_norm_gain[l]) * (1.0 - lam_init)).reshape(B, S, DIFF_WIDTH).astype(x.dtype)

        gate_a, gate_b = jnp.split(gates, 2, axis=-1)
        merged = jax.nn.sigmoid(gate_a) * (o_a @ w_branch_a[l]) + jax.nn.sigmoid(gate_b) * (o_b @ w_branch_b[l])
        x = x + merged @ w_out[l]

        h2 = rms_norm(x, norm_ffn_gain[l])
        x = x + peer_ffn(h2, peer_w_q[l], peer_sub_keys[l], peer_u[l], peer_v[l])
    return rms_norm(x, norm_final_gain)


def setup_inputs(seed: int = 0) -> dict:
    key = jax.random.key(seed)
    ks = jax.random.split(key, 20)
    f32 = jnp.float32

    def normal(k, shape, scale):
        return jax.random.normal(k, shape, f32) * scale

    def gain(k, shape):
        return 1.0 + 0.02 * jax.random.normal(k, shape, f32)

    dt = jnp.exp(jax.random.uniform(ks[6], (DEPTH, 2, GDN_HEADS), f32, math.log(1e-3), math.log(1e-1)))
    return {
        'x_prompt': normal(ks[0], (BATCH, SEQ, D_MODEL), 1.0),
        'x_sample': normal(ks[1], (DEC_BATCH, DEC_SEQ, D_MODEL), 1.0),
        'norm_mix_gain': gain(ks[2], (DEPTH, D_MODEL)),
        'w_in': normal(ks[3], (DEPTH, D_MODEL, IN_COLS), D_MODEL ** -0.5),
        'conv_w': normal(ks[4], (DEPTH, GDN_CONV, GDN_QKV), GDN_CONV ** -0.5),
        'gdn_a_log': jnp.log(jax.random.uniform(ks[5], (DEPTH, 2, GDN_HEADS), f32, 1.0, 16.0)),
        'gdn_dt_bias': dt + jnp.log(-jnp.expm1(-dt)),
        'gdn_norm_gain': gain(ks[7], (DEPTH, GDN_DV)),
        'diff_lambda': normal(ks[8], (DEPTH, 4, DIFF_HD), 0.1),
        'diff_norm_gain': gain(ks[9], (DEPTH, 2 * DIFF_HD)),
        'w_branch_a': normal(ks[10], (DEPTH, GDN_WIDTH, D_MODEL), GDN_WIDTH ** -0.5),
        'w_branch_b': normal(ks[11], (DEPTH, DIFF_WIDTH, D_MODEL), DIFF_WIDTH ** -0.5),
        'w_out': normal(ks[12], (DEPTH, D_MODEL, D_MODEL), D_MODEL ** -0.5),
        'norm_ffn_gain': gain(ks[13], (DEPTH, D_MODEL)),
        'peer_w_q': normal(ks[14], (DEPTH, D_MODEL, PEER_HEADS * PEER_DQ), D_MODEL ** -0.5),
        'peer_sub_keys': normal(ks[15], (DEPTH, PEER_HEADS, 2, PEER_NKEYS, PEER_DQ // 2), (PEER_DQ // 2) ** -0.5),
        'peer_u': normal(ks[16], (DEPTH, PEER_EXPERTS, D_MODEL), D_MODEL ** -0.5),
        'peer_v': normal(ks[17], (DEPTH, PEER_EXPERTS, D_MODEL), 0.5),
        'norm_final_gain': gain(ks[18], (D_MODEL,)),
    }


def reference(x_prompt, x_sample, norm_mix_gain, w_in, conv_w, gdn_a_log, gdn_dt_bias, gdn_norm_gain,
              diff_lambda, diff_norm_gain, w_branch_a, w_branch_b, w_out, norm_ffn_gain,
              peer_w_q, peer_sub_keys, peer_u, peer_v, norm_final_gain):
    y_prompt = encoder_trunk(x_prompt, norm_mix_gain, w_in, conv_w, gdn_a_log, gdn_dt_bias, gdn_norm_gain,
                             diff_lambda, diff_norm_gain, w_branch_a, w_branch_b, w_out, norm_ffn_gain,
                             peer_w_q, peer_sub_keys, peer_u, peer_v, norm_final_gain)
    y_sample = encoder_trunk(x_sample, norm_mix_gain, w_in, conv_w, gdn_a_log, gdn_dt_bias, gdn_norm_gain,
                             diff_lambda, diff_norm_gain, w_branch_a, w_branch_b, w_out, norm_ffn_gain,
                             peer_w_q, peer_sub_keys, peer_u, peer_v, norm_final_gain)
    return (y_prompt, y_sample)
```

```python
import functools
import math

import jax
import jax.numpy as jnp
from jax import lax
from jax.experimental import pallas as pl
from jax.experimental.pallas import tpu as pltpu

F32 = jnp.float32
BF16 = jnp.bfloat16

D_MODEL = 2048
GDN_HEADS = 8
GDN_DK = 128
GDN_QK = 1024
GDN_WIDTH = 1024
GDN_QKV = 3072
GDN_CONV = 5
GDN_CHUNK = 128
DIFF_HEADS = 4
DIFF_HD = 128
DIFF_QK = 1024
DIFF_WIDTH = 1024
PEER_HEADS = 8
PEER_NKEYS = 128
PEER_EXPERTS = PEER_NKEYS * PEER_NKEYS
PEER_DQ = 256
PEER_TOPK = 16
NORM_EPS = 1e-6
LAM_INIT = 0.8 - 0.6 * math.exp(-0.3 * 0)

LANES = 128
SUBLANES = 8
VMEM_LIMIT_BYTES = 56 * 1024 * 1024

NEG_INF = float("-inf")


def _params(*sem):
    return pltpu.CompilerParams(dimension_semantics=sem, vmem_limit_bytes=VMEM_LIMIT_BYTES)


def _tile(n, pref):
    t = min(n, pref)
    while n % t:
        t //= 2
    return t


def _dot(a, b):
    return jnp.dot(a, b, preferred_element_type=F32)


def _dot_nt(a, b):
    return lax.dot_general(a, b, (((1,), (1,)), ((), ())), preferred_element_type=F32)


def _dot_tn(a, b):
    return lax.dot_general(a, b, (((0,), (0,)), ((), ())), preferred_element_type=F32)


def _sigmoid(x):
    return 1.0 / (1.0 + jnp.exp(-x))


def _rms(x, gain):
    return x * lax.rsqrt(jnp.mean(x * x, axis=-1, keepdims=True) + NORM_EPS) * gain


def _norm_matmul_kernel(x_ref, g_ref, w_ref, *rest, epilogue, n_extra):
    extra, o_ref, h_ref = rest[:n_extra], rest[n_extra], rest[n_extra + 1]

    @pl.when(pl.program_id(1) == 0)
    def _():
        h_ref[...] = _rms(x_ref[...], g_ref[...]).astype(BF16)

    acc = _dot(h_ref[...], w_ref[...])
    if epilogue is not None:
        acc = epilogue(acc, *[e[...] for e in extra])
    o_ref[...] = acc.astype(o_ref.dtype)


def norm_matmul(x, gain, w, out_dtype, *, epilogue=None, extra=(), tm=1024, tn=1024, name):
    t, d = x.shape
    n = w.shape[1]
    tm, tn = _tile(t, tm), _tile(n, tn)
    kern = functools.partial(_norm_matmul_kernel, epilogue=epilogue, n_extra=len(extra))
    return pl.pallas_call(
        kern,
        grid=(t // tm, n // tn),
        in_specs=[pl.BlockSpec((tm, d), lambda i, j: (i, 0)),
                  pl.BlockSpec((1, d), lambda i, j: (0, 0)),
                  pl.BlockSpec((d, tn), lambda i, j: (0, j))]
                 + [pl.BlockSpec((1, tn), lambda i, j: (0, j)) for _ in extra],
        out_specs=pl.BlockSpec((tm, tn), lambda i, j: (i, j)),
        out_shape=jax.ShapeDtypeStruct((t, n), out_dtype),
        scratch_shapes=[pltpu.VMEM((tm, d), BF16)],
        compiler_params=_params("parallel", "arbitrary"),
        name=name,
    )(x, gain.reshape(1, d), w, *extra)


def _gate_epilogue(acc, alog, dtb):
    z = acc + dtb
    softplus = jnp.maximum(z, 0.0) + jnp.log(1.0 + jnp.exp(-jnp.abs(z)))
    g = -jnp.exp(alog) * softplus
    lane = lax.broadcasted_iota(jnp.int32, acc.shape, 1)
    return jnp.where(lane < 2 * GDN_HEADS, g, _sigmoid(acc))


def _matmul_res_kernel(a_ref, w_ref, r_ref, o_ref):
    o_ref[...] = r_ref[...] + _dot(a_ref[...], w_ref[...])


def matmul_residual(a, w, res, *, tm=1024, tn=1024, name):
    t, k = a.shape
    n = w.shape[1]
    tm, tn = _tile(t, tm), _tile(n, tn)
    return pl.pallas_call(
        _matmul_res_kernel,
        grid=(t // tm, n // tn),
        in_specs=[pl.BlockSpec((tm, k), lambda i, j: (i, 0)),
                  pl.BlockSpec((k, tn), lambda i, j: (0, j)),
                  pl.BlockSpec((tm, tn), lambda i, j: (i, j))],
        out_specs=pl.BlockSpec((tm, tn), lambda i, j: (i, j)),
        out_shape=jax.ShapeDtypeStruct((t, n), F32),
        compiler_params=_params("parallel", "arbitrary"),
        name=name,
    )(a, w, res)


def _conv_kernel(cur_ref, prev_ref, next_ref, w_ref, o_ref, pad_ref):
    i, c = pl.program_id(1), pl.program_id(2)
    ts = cur_ref.shape[1]
    halo = SUBLANES
    pad_ref[0:halo, :] = jnp.where(i > 0, prev_ref[0], 0.0)
    pad_ref[halo:halo + ts, :] = cur_ref[0]
    pad_ref[halo + ts:2 * halo + ts, :] = jnp.where(i < pl.num_programs(1) - 1, next_ref[0], 0.0)
    w = w_ref[...]
    first = halo - (GDN_CONV - 1) // 2
    acc = w[0:1, :] * pad_ref[first:first + ts, :]
    for k in range(1, GDN_CONV):
        acc = acc + w[k:k + 1, :] * pad_ref[first + k:first + k + ts, :]
    y = acc * _sigmoid(acc)
    rs = lax.rsqrt(jnp.sum(y * y, axis=-1, keepdims=True) + 1e-6)
    f = jnp.where(c < GDN_HEADS, rs * (GDN_DK ** -0.5), jnp.where(c < 2 * GDN_HEADS, rs, 1.0))
    o_ref[0] = (y * f).astype(o_ref.dtype)


def conv_prep(proj_a, conv_w, *, ts=2048):
    b, s, _ = proj_a.shape
    ts = _tile(s, ts)
    r = ts // SUBLANES
    last = s // SUBLANES - 1
    return pl.pallas_call(
        _conv_kernel,
        grid=(b, s // ts, GDN_QKV // LANES),
        in_specs=[pl.BlockSpec((1, ts, LANES), lambda b_, i, c: (b_, i, c)),
                  pl.BlockSpec((1, SUBLANES, LANES), lambda b_, i, c: (b_, jnp.maximum(i * r - 1, 0), c)),
                  pl.BlockSpec((1, SUBLANES, LANES), lambda b_, i, c: (b_, jnp.minimum((i + 1) * r, last), c)),
                  pl.BlockSpec((GDN_CONV, LANES), lambda b_, i, c: (0, c))],
        out_specs=pl.BlockSpec((1, ts, LANES), lambda b_, i, c: (b_, i, c)),
        out_shape=jax.ShapeDtypeStruct((b, s, GDN_QKV), BF16),
        scratch_shapes=[pltpu.VMEM((ts + 2 * SUBLANES, LANES), F32)],
        compiler_params=_params("parallel", "parallel", "parallel"),
        name="gdn_conv_prep",
    )(proj_a, proj_a, proj_a, conv_w)


def _split3(x):
    hi = x.astype(BF16)
    r = x - hi.astype(F32)
    mid = r.astype(BF16)
    lo = (r - mid.astype(F32)).astype(BF16)
    return hi, mid, lo


def _gdn_kernel(q_ref, k_ref, v_ref, gc_ref, bc_ref, gr_ref, o_ref, state_ref):
    d, i = pl.program_id(1), pl.program_id(2)
    c = GDN_CHUNK

    @pl.when(i == 0)
    def _():
        state_ref[...] = jnp.zeros_like(state_ref)

    row = lax.broadcasted_iota(jnp.int32, (c, c), 0)
    col = lax.broadcasted_iota(jnp.int32, (c, c), 1)
    order = jnp.where(d == 0, row - col, col - row)
    incl = order >= 0
    strict = order > 0
    eye = jnp.where(row == col, 1.0, 0.0)
    ones_incl = jnp.where(incl, 1.0, 0.0).astype(BF16)

    g_col = gc_ref[0, 0]
    g_row = gr_ref[0, 0]
    gcum_col = sum(_dot(ones_incl, p) for p in _split3(g_col))
    gcum_row = sum(_dot_nt(p, ones_incl) for p in _split3(g_row))
    g_tot = jnp.sum(g_col, axis=0, keepdims=True)
    beta_all = bc_ref[0, 0]

    for h in range(GDN_HEADS):
        hs = slice(h * GDN_DK, (h + 1) * GDN_DK)
        q = q_ref[0, :, hs]
        k = k_ref[0, :, hs]
        v = v_ref[0, :, hs]
        beta = beta_all[:, h:h + 1]
        gcc = gcum_col[:, h:h + 1]
        gcr = gcum_row[h:h + 1, :]
        gt = g_tot[:, h:h + 1]

        kf = k.astype(F32)
        kb = kf * beta
        decay = jnp.where(incl, jnp.exp(jnp.where(incl, gcc - gcr, 0.0)), 0.0)
        a = jnp.where(strict, _dot_nt(kb.astype(BF16), k) * decay, 0.0)

        p = -a
        t_inv = eye + p
        n_levels = int(math.log2(c)) - 1
        for _ in range(n_levels):
            pb = p.astype(BF16)
            p = _dot(pb, pb)
            t_inv = t_inv + _dot(t_inv.astype(BF16), p.astype(BF16))
        tb = t_inv.astype(BF16)

        eg = jnp.exp(gcc)
        u = _dot(tb, (v.astype(F32) * beta).astype(BF16))
        w = _dot(tb, (kb * eg).astype(BF16))
        qk = jnp.where(incl, _dot_nt(q, k) * decay, 0.0)
        q_dec = q.astype(F32) * eg
        k_tail = kf * jnp.exp(gt - gcc)

        s = state_ref[h]
        sb = s.astype(BF16)
        v_new = u - _dot(w.astype(BF16), sb)
        vb = v_new.astype(BF16)
        o = _dot(q_dec.astype(BF16), sb) + _dot(qk.astype(BF16), vb)
        state_ref[h] = s * jnp.exp(gt) + _dot_tn(k_tail.astype(BF16), vb)
        o_ref[0, 0, :, hs] = o


def gdn_scan(qkv_n, g_col, beta_col, g_row):
    b, s, _ = qkv_n.shape
    c = GDN_CHUNK
    nc = s // c

    def cidx(d, i):
        return i + d * (nc - 1 - 2 * i)

    def qkv_spec(part):
        return pl.BlockSpec((1, c, GDN_QK), lambda b_, d, i: (b_, cidx(d, i), part))

    col_spec = pl.BlockSpec((1, 1, c, GDN_HEADS), lambda b_, d, i: (b_, d, cidx(d, i), 0))
    row_spec = pl.BlockSpec((1, 1, GDN_HEADS, c), lambda b_, d, i: (b_, d, 0, cidx(d, i)))
    return pl.pallas_call(
        _gdn_kernel,
        grid=(b, 2, nc),
        in_specs=[qkv_spec(0), qkv_spec(1), qkv_spec(2), col_spec, col_spec, row_spec],
        out_specs=pl.BlockSpec((1, 1, c, GDN_WIDTH), lambda b_, d, i: (d, b_, cidx(d, i), 0)),
        out_shape=jax.ShapeDtypeStruct((2, b, s, GDN_WIDTH), F32),
        scratch_shapes=[pltpu.VMEM((GDN_HEADS, GDN_DK, GDN_DK), F32)],
        compiler_params=_params("parallel", "parallel", "arbitrary"),
        name="gdn_scan",
    )(qkv_n, qkv_n, qkv_n, g_col, beta_col, g_row)


def _attn_kernel(slopes_ref, q_ref, k_ref, v_ref, lam_ref, gain_ref, o_ref, m_ref, l_ref, acc_ref):
    h, qi, ki = pl.program_id(1), pl.program_id(2), pl.program_id(3)
    tq, tk = q_ref.shape[1], k_ref.shape[1]

    @pl.when(ki == 0)
    def _():
        m_ref[...] = jnp.full_like(m_ref, NEG_INF)
        l_ref[...] = jnp.zeros_like(l_ref)
        acc_ref[...] = jnp.zeros_like(acc_ref)

    qpos = qi * tq + lax.broadcasted_iota(jnp.int32, (tq, tk), 0)
    kpos = ki * tk + lax.broadcasted_iota(jnp.int32, (tq, tk), 1)
    bias = slopes_ref[h] * jnp.abs(qpos - kpos).astype(F32)
    v = v_ref[0]
    for c in range(2):
        cs = slice(c * DIFF_HD, (c + 1) * DIFF_HD)
        s = _dot_nt(q_ref[0, :, cs], k_ref[0, :, cs]) * (DIFF_HD ** -0.5) - bias
        m_prev = m_ref[c]
        m_new = jnp.maximum(m_prev, jnp.max(s, axis=-1, keepdims=True))
        alpha = jnp.exp(m_prev - m_new)
        p = jnp.exp(s - m_new[:, 0:1])
        l_ref[c] = alpha * l_ref[c] + jnp.sum(p, axis=-1, keepdims=True)
        acc_ref[c] = acc_ref[c] * alpha[:, 0:1] + _dot(p.astype(BF16), v)
        m_ref[c] = m_new

    @pl.when(ki == pl.num_programs(3) - 1)
    def _():
        lp = lam_ref[...]
        lam = (jnp.exp(jnp.sum(lp[0:1] * lp[1:2], axis=-1, keepdims=True))
               - jnp.exp(jnp.sum(lp[2:3] * lp[3:4], axis=-1, keepdims=True)) + LAM_INIT)
        o = acc_ref[0] / l_ref[0][:, 0:1] - lam * (acc_ref[1] / l_ref[1][:, 0:1])
        o_ref[0] = (_rms(o, gain_ref[...]) * (1.0 - LAM_INIT)).astype(o_ref.dtype)


def diff_attention(proj_b, slopes, diff_lambda, gain, *, tq=512, tk=512):
    b, s, _ = proj_b.shape
    tq, tk = _tile(s, tq), _tile(s, tk)
    e = 2 * DIFF_HD
    return pl.pallas_call(
        _attn_kernel,
        grid=(b, DIFF_HEADS, s // tq, s // tk),
        in_specs=[pl.BlockSpec(memory_space=pltpu.SMEM),
                  pl.BlockSpec((1, tq, e), lambda b_, h, qi, ki: (b_, qi, h)),
                  pl.BlockSpec((1, tk, e), lambda b_, h, qi, ki: (b_, ki, DIFF_HEADS + h)),
                  pl.BlockSpec((1, tk, e), lambda b_, h, qi, ki: (b_, ki, 2 * DIFF_HEADS + h)),
                  pl.BlockSpec((4, DIFF_HD), lambda b_, h, qi, ki: (0, 0)),
                  pl.BlockSpec((1, e), lambda b_, h, qi, ki: (0, 0))],
        out_specs=pl.BlockSpec((1, tq, e), lambda b_, h, qi, ki: (b_, qi, h)),
        out_shape=jax.ShapeDtypeStruct((b, s, DIFF_WIDTH), BF16),
        scratch_shapes=[pltpu.VMEM((2, tq, LANES), F32),
                        pltpu.VMEM((2, tq, LANES), F32),
                        pltpu.VMEM((2, tq, e), F32)],
        compiler_params=_params("parallel", "parallel", "parallel", "arbitrary"),
        name="diff_attention",
    )(slopes, proj_b, proj_b, proj_b, diff_lambda, gain.reshape(1, e))


def _merge_kernel(o_ref, z_ref, ga_ref, gb_ref, ob_ref, wa_ref, wb_ref, gain_ref, out_ref, oa_ref):
    @pl.when(pl.program_id(1) == 0)
    def _():
        gain = gain_ref[...]
        for h in range(GDN_HEADS):
            hs = slice(h * GDN_DK, (h + 1) * GDN_DK)
            z = z_ref[:, hs]
            oa_ref[:, hs] = (_rms(o_ref[0, :, hs] + o_ref[1, :, hs], gain) * (z * _sigmoid(z))).astype(BF16)

    out_ref[...] = (_sigmoid(ga_ref[...]) * _dot(oa_ref[...], wa_ref[...])
                    + _sigmoid(gb_ref[...]) * _dot(ob_ref[...], wb_ref[...])).astype(out_ref.dtype)


def merge_branches(o_gdn, proj_a, o_b, w_a, w_b, gain, *, tm=512, tn=1024):
    t = o_b.shape[0]
    tm = _tile(t, tm)
    z_blk = GDN_QKV // GDN_WIDTH
    ga_blk = (GDN_QKV + GDN_WIDTH) // tn
    gb_blk = (GDN_QKV + GDN_WIDTH + D_MODEL) // tn
    return pl.pallas_call(
        _merge_kernel,
        grid=(t // tm, D_MODEL // tn),
        in_specs=[pl.BlockSpec((2, tm, GDN_WIDTH), lambda i, j: (0, i, 0)),
                  pl.BlockSpec((tm, GDN_WIDTH), lambda i, j: (i, z_blk)),
                  pl.BlockSpec((tm, tn), lambda i, j: (i, ga_blk + j)),
                  pl.BlockSpec((tm, tn), lambda i, j: (i, gb_blk + j)),
                  pl.BlockSpec((tm, DIFF_WIDTH), lambda i, j: (i, 0)),
                  pl.BlockSpec((GDN_WIDTH, tn), lambda i, j: (0, j)),
                  pl.BlockSpec((DIFF_WIDTH, tn), lambda i, j: (0, j)),
                  pl.BlockSpec((1, GDN_DK), lambda i, j: (0, 0))],
        out_specs=pl.BlockSpec((tm, tn), lambda i, j: (i, j)),
        out_shape=jax.ShapeDtypeStruct((t, D_MODEL), BF16),
        scratch_shapes=[pltpu.VMEM((tm, GDN_WIDTH), BF16)],
        compiler_params=_params("parallel", "arbitrary"),
        name="merge_branches",
    )(o_gdn, proj_a, proj_a, proj_a, o_b, w_a, w_b, gain.reshape(1, GDN_DK))


def _extract_distinct(x, n):
    vals, cnts = [], []
    for _ in range(n):
        m = jnp.max(x, axis=0, keepdims=True)
        eq = x == m
        cnt = jnp.sum(jnp.where(eq, 1.0, 0.0), axis=0, keepdims=True)
        vals.append(m)
        cnts.append(jnp.where(m == NEG_INF, 0.0, cnt))
        x = jnp.where(eq, NEG_INF, x)
    return vals, cnts


def _route_kernel(q_ref, keys_ref, a_ref, b_ref, e1_ref, e2_ref, tau_ref):
    half = PEER_DQ // 2
    s1 = _dot_nt(keys_ref[0, 0], q_ref[:, 0:half])
    s2 = _dot_nt(keys_ref[0, 1], q_ref[:, half:PEER_DQ])
    v1, n1 = _extract_distinct(s1, PEER_TOPK)
    v2, n2 = _extract_distinct(s2, PEER_TOPK)
    a = s1 - v1[0]
    b = s2 - v2[0]
    bt = jnp.concatenate([v - v2[0] for v in v2], axis=0)
    nb = jnp.concatenate(n2, axis=0)
    cand = jnp.concatenate([(v - v1[0]) + bt for v in v1], axis=0)
    mult = jnp.concatenate([n * nb for n in n1], axis=0)

    x = cand
    tau = jnp.zeros_like(v1[0])
    cum = jnp.zeros_like(v1[0])
    for _ in range(PEER_TOPK):
        m = jnp.max(x, axis=0, keepdims=True)
        eq = x == m
        cnt = jnp.sum(jnp.where(eq, mult, 0.0), axis=0, keepdims=True)
        tau = jnp.where(cum < PEER_TOPK, m, tau)
        cum = cum + cnt
        x = jnp.where(eq, NEG_INF, x)

    z = jnp.sum(jnp.where(cand >= tau, mult * jnp.exp(cand), 0.0), axis=0, keepdims=True)
    a_ref[0] = a
    b_ref[0] = b
    e1_ref[0] = jnp.exp(a)
    e2_ref[0] = jnp.exp(b) / z
    tau_ref[0] = tau


def peer_route(q, keys, *, tt=256):
    t = q.shape[0]
    tt = _tile(t, tt)
    tab = jax.ShapeDtypeStruct((PEER_HEADS, PEER_NKEYS, t), F32)
    tab_spec = pl.BlockSpec((1, PEER_NKEYS, tt), lambda i, p: (p, 0, i))
    return pl.pallas_call(
        _route_kernel,
        grid=(t // tt, PEER_HEADS),
        in_specs=[pl.BlockSpec((tt, PEER_DQ), lambda i, p: (i, p)),
                  pl.BlockSpec((1, 2, PEER_NKEYS, PEER_DQ // 2), lambda i, p: (p, 0, 0, 0))],
        out_specs=[tab_spec, tab_spec, tab_spec, tab_spec,
                   pl.BlockSpec((1, 1, tt), lambda i, p: (p, 0, i))],
        out_shape=[tab, tab, tab, tab, jax.ShapeDtypeStruct((PEER_HEADS, 1, t), F32)],
        compiler_params=_params("parallel", "parallel"),
        name="peer_route",
    )(q, keys)


def _gelu_tanh(x):
    return x * (0.5 * (1.0 + jnp.tanh(math.sqrt(2.0 / math.pi) * (x + 0.044715 * (x * x * x)))))


def _peer_kernel(x_ref, gf_ref, u_ref, v_ref, a_ref, b_ref, e1_ref, e2_ref, tau_ref, gfin_ref,
                 o_ref, h_ref, acc_ref, coef_ref):
    e = pl.program_id(1)
    te = u_ref.shape[0]
    rows = te // PEER_NKEYS

    @pl.when(e == 0)
    def _():
        h_ref[...] = _rms(x_ref[...], gf_ref[...]).astype(BF16)
        acc_ref[...] = jnp.zeros_like(acc_ref)

    act = _gelu_tanh(_dot_nt(u_ref[...], h_ref[...]))
    for il in range(rows):
        i = e * rows + il
        g = None
        for p in range(PEER_HEADS):
            pair = a_ref[p, pl.ds(i, 1), :] + b_ref[p]
            w = jnp.where(pair >= tau_ref[p], e1_ref[p, pl.ds(i, 1), :] * e2_ref[p], 0.0)
            g = w if g is None else g + w
        rs = slice(il * PEER_NKEYS, (il + 1) * PEER_NKEYS)
        coef_ref[rs, :] = (act[rs, :] * g).astype(BF16)
    acc_ref[...] += _dot_tn(coef_ref[...], v_ref[...])

    @pl.when(e == pl.num_programs(1) - 1)
    def _():
        o_ref[...] = _rms(x_ref[...] + acc_ref[...], gfin_ref[...])


def peer_experts(x1, gain_ffn, u, v, tabs, tau, gain_final, *, tt=512, te=512):
    t, d = x1.shape
    tt, te = _tile(t, tt), _tile(PEER_EXPERTS, te)
    tab_spec = pl.BlockSpec((PEER_HEADS, PEER_NKEYS, tt), lambda i, e: (0, 0, i))
    return pl.pallas_call(
        _peer_kernel,
        grid=(t // tt, PEER_EXPERTS // te),
        in_specs=[pl.BlockSpec((tt, d), lambda i, e: (i, 0)),
                  pl.BlockSpec((1, d), lambda i, e: (0, 0)),
                  pl.BlockSpec((te, d), lambda i, e: (e, 0)),
                  pl.BlockSpec((te, d), lambda i, e: (e, 0)),
                  tab_spec, tab_spec, tab_spec, tab_spec,
                  pl.BlockSpec((PEER_HEADS, 1, tt), lambda i, e: (0, 0, i)),
                  pl.BlockSpec((1, d), lambda i, e: (0, 0))],
        out_specs=pl.BlockSpec((tt, d), lambda i, e: (i, 0)),
        out_shape=jax.ShapeDtypeStruct((t, d), F32),
        scratch_shapes=[pltpu.VMEM((tt, d), BF16),
                        pltpu.VMEM((tt, d), F32),
                        pltpu.VMEM((te, tt), BF16)],
        compiler_params=_params("parallel", "arbitrary"),
        name="peer_experts",
    )(x1, gain_ffn.reshape(1, d), u, v, *tabs, tau, gain_final.reshape(1, d))


def _prepare_weights(w_in, gdn_a_log, gdn_dt_bias, w_branch_a, w_branch_b, w_out, peer_w_q,
                     peer_sub_keys, peer_u, peer_v):
    o_z = GDN_QKV + GDN_WIDTH
    o_ab = o_z + 4 * GDN_HEADS
    o_b = o_ab + 2 * DIFF_QK + DIFF_WIDTH
    w = w_in[0]
    pad = jnp.zeros((D_MODEL, LANES - 4 * GDN_HEADS), F32)
    row = lambda p: jnp.concatenate([p[0].reshape(1, -1), jnp.zeros((1, LANES - 2 * GDN_HEADS), F32)], axis=1)
    return dict(
        w_a=jnp.concatenate([w[:, :o_z], w[:, o_b:]], axis=1).astype(BF16),
        w_ab=jnp.concatenate([w[:, o_z:o_ab], pad], axis=1).astype(BF16),
        w_b=w[:, o_ab:o_b].astype(BF16),
        alog=row(gdn_a_log), dtb=row(gdn_dt_bias),
        w_branch_a=w_branch_a[0].astype(BF16), w_branch_b=w_branch_b[0].astype(BF16),
        w_out=w_out[0].astype(BF16), w_q=peer_w_q[0].astype(BF16),
        keys=peer_sub_keys[0].astype(BF16), u=peer_u[0].astype(BF16), v=peer_v[0].astype(BF16),
    )


def _trunk(x, pw, norm_mix_gain, conv_w, gdn_norm_gain, diff_lambda, diff_norm_gain,
           norm_ffn_gain, norm_final_gain, slopes):
    b, s, d = x.shape
    t = b * s
    xt = x.reshape(t, d)
    g_mix = norm_mix_gain[0]

    proj_a = norm_matmul(xt, g_mix, pw["w_a"], F32, name="in_proj_a")
    proj_b = norm_matmul(xt, g_mix, pw["w_b"], BF16, name="in_proj_b")
    gb = norm_matmul(xt, g_mix, pw["w_ab"], F32, epilogue=_gate_epilogue,
                     extra=(pw["alog"], pw["dtb"]), name="in_proj_gates")

    qkv_n = conv_prep(proj_a.reshape(b, s, -1), conv_w[0])
    g = gb[:, :2 * GDN_HEADS].reshape(b, s, 2, GDN_HEADS)
    beta = gb[:, 2 * GDN_HEADS:4 * GDN_HEADS].reshape(b, s, 2, GDN_HEADS)
    g_col = jnp.transpose(g, (0, 2, 1, 3))
    beta_col = jnp.transpose(beta, (0, 2, 1, 3))
    g_row = jnp.transpose(g, (0, 2, 3, 1))
    o_gdn = gdn_scan(qkv_n, g_col, beta_col, g_row).reshape(2, t, GDN_WIDTH)

    o_b = diff_attention(proj_b.reshape(b, s, -1), slopes, diff_lambda[0], diff_norm_gain[0])
    o_b = o_b.reshape(t, DIFF_WIDTH)

    merged = merge_branches(o_gdn, proj_a, o_b, pw["w_branch_a"], pw["w_branch_b"], gdn_norm_gain[0])
    x1 = matmul_residual(merged, pw["w_out"], xt, name="out_proj")

    q = norm_matmul(x1, norm_ffn_gain[0], pw["w_q"], BF16, name="peer_query")
    a, bb, e1, e2, tau = peer_route(q, pw["keys"])
    y = peer_experts(x1, norm_ffn_gain[0], pw["u"], pw["v"], (a, bb, e1, e2), tau, norm_final_gain)
    return y.reshape(b, s, d)


def kernel(x_prompt, x_sample, norm_mix_gain, w_in, conv_w, gdn_a_log, gdn_dt_bias, gdn_norm_gain,
           diff_lambda, diff_norm_gain, w_branch_a, w_branch_b, w_out, norm_ffn_gain,
           peer_w_q, peer_sub_keys, peer_u, peer_v, norm_final_gain):
    pw = _prepare_weights(w_in, gdn_a_log, gdn_dt_bias, w_branch_a, w_branch_b, w_out, peer_w_q,
                          peer_sub_keys, peer_u, peer_v)
    slopes = 2.0 ** (-8.0 * jnp.arange(1, DIFF_HEADS + 1, dtype=F32) / DIFF_HEADS)
    run = functools.partial(_trunk, pw=pw, norm_mix_gain=norm_mix_gain, conv_w=conv_w,
                            gdn_norm_gain=gdn_norm_gain, diff_lambda=diff_lambda,
                            diff_norm_gain=diff_norm_gain, norm_ffn_gain=norm_ffn_gain,
                            norm_final_gain=norm_final_gain, slopes=slopes)
    return (run(x_prompt), run(x_sample))
```

```python
import functools
import math

import jax
import jax.numpy as jnp
from jax import lax
from jax.experimental import pallas as pl
from jax.experimental.pallas import tpu as pltpu

F32 = jnp.float32
BF16 = jnp.bfloat16

D_MODEL = 2048
GDN_HEADS = 8
GDN_DK = 128
GDN_QK = 1024
GDN_WIDTH = 1024
GDN_QKV = 3072
GDN_CONV = 5
GDN_CHUNK = 128
DIFF_HEADS = 4
DIFF_HD = 128
DIFF_QK = 1024
DIFF_WIDTH = 1024
PEER_HEADS = 8
PEER_NKEYS = 128
PEER_EXPERTS = PEER_NKEYS * PEER_NKEYS
PEER_DQ = 256
PEER_TOPK = 16
NORM_EPS = 1e-6
LAM_INIT = 0.8 - 0.6 * math.exp(-0.3 * 0)

LANES = 128
SUBLANES = 8
VMEM_LIMIT_BYTES = 56 * 1024 * 1024

NEG_INF = float("-inf")
LOG2E = math.log2(math.e)


def _params(*sem):
    return pltpu.CompilerParams(dimension_semantics=sem, vmem_limit_bytes=VMEM_LIMIT_BYTES)


def _tile(n, pref):
    t = min(n, pref)
    while n % t:
        t //= 2
    return t


def _dot(a, b):
    return jnp.dot(a, b, preferred_element_type=F32)


def _dot_nt(a, b):
    return lax.dot_general(a, b, (((1,), (1,)), ((), ())), preferred_element_type=F32)


def _dot_tn(a, b):
    return lax.dot_general(a, b, (((0,), (0,)), ((), ())), preferred_element_type=F32)


def _sigmoid(x):
    return 1.0 / (1.0 + jnp.exp(-x))


def _rms(x, gain):
    return x * lax.rsqrt(jnp.mean(x * x, axis=-1, keepdims=True) + NORM_EPS) * gain


def _norm_matmul_kernel(x_ref, g_ref, w_ref, *rest, epilogue, n_extra):
    extra, o_ref, h_ref = rest[:n_extra], rest[n_extra], rest[n_extra + 1]

    @pl.when(pl.program_id(1) == 0)
    def _():
        h_ref[...] = _rms(x_ref[...], g_ref[...]).astype(BF16)

    acc = _dot(h_ref[...], w_ref[...])
    if epilogue is not None:
        acc = epilogue(acc, *[e[...] for e in extra])
    o_ref[...] = acc.astype(o_ref.dtype)


def norm_matmul(x, gain, w, out_dtype, *, epilogue=None, extra=(), tm=1024, tn=1024, name):
    t, d = x.shape
    n = w.shape[1]
    tm, tn = _tile(t, tm), _tile(n, tn)
    kern = functools.partial(_norm_matmul_kernel, epilogue=epilogue, n_extra=len(extra))
    return pl.pallas_call(
        kern,
        grid=(t // tm, n // tn),
        in_specs=[pl.BlockSpec((tm, d), lambda i, j: (i, 0)),
                  pl.BlockSpec((1, d), lambda i, j: (0, 0)),
                  pl.BlockSpec((d, tn), lambda i, j: (0, j))]
                 + [pl.BlockSpec((1, tn), lambda i, j: (0, j)) for _ in extra],
        out_specs=pl.BlockSpec((tm, tn), lambda i, j: (i, j)),
        out_shape=jax.ShapeDtypeStruct((t, n), out_dtype),
        scratch_shapes=[pltpu.VMEM((tm, d), BF16)],
        compiler_params=_params("parallel", "arbitrary"),
        name=name,
    )(x, gain.reshape(1, d), w, *extra)


def _gate_epilogue(acc, alog, dtb):
    z = acc + dtb
    softplus = jnp.maximum(z, 0.0) + jnp.log(1.0 + jnp.exp(-jnp.abs(z)))
    g = -jnp.exp(alog) * softplus
    lane = lax.broadcasted_iota(jnp.int32, acc.shape, 1)
    return jnp.where(lane < 2 * GDN_HEADS, g, _sigmoid(acc))


def _scale_epilogue(acc, scale):
    return acc * scale


def _matmul_res_kernel(a_ref, w_ref, r_ref, o_ref):
    o_ref[...] = r_ref[...] + _dot(a_ref[...], w_ref[...])


def matmul_residual(a, w, res, *, tm=1024, tn=1024, name):
    t, k = a.shape
    n = w.shape[1]
    tm, tn = _tile(t, tm), _tile(n, tn)
    return pl.pallas_call(
        _matmul_res_kernel,
        grid=(t // tm, n // tn),
        in_specs=[pl.BlockSpec((tm, k), lambda i, j: (i, 0)),
                  pl.BlockSpec((k, tn), lambda i, j: (0, j)),
                  pl.BlockSpec((tm, tn), lambda i, j: (i, j))],
        out_specs=pl.BlockSpec((tm, tn), lambda i, j: (i, j)),
        out_shape=jax.ShapeDtypeStruct((t, n), F32),
        compiler_params=_params("parallel", "arbitrary"),
        name=name,
    )(a, w, res)


def _conv_kernel(cur_ref, prev_ref, next_ref, w_ref, o_ref, pad_ref):
    i, c = pl.program_id(1), pl.program_id(2)
    ts = cur_ref.shape[1]
    halo = SUBLANES
    pad_ref[0:halo, :] = jnp.where(i > 0, prev_ref[0], 0.0)
    pad_ref[halo:halo + ts, :] = cur_ref[0]
    pad_ref[halo + ts:2 * halo + ts, :] = jnp.where(i < pl.num_programs(1) - 1, next_ref[0], 0.0)
    w = w_ref[...]
    first = halo - (GDN_CONV - 1) // 2
    acc = w[0:1, :] * pad_ref[first:first + ts, :]
    for k in range(1, GDN_CONV):
        acc = acc + w[k:k + 1, :] * pad_ref[first + k:first + k + ts, :]
    y = acc * _sigmoid(acc)
    rs = lax.rsqrt(jnp.sum(y * y, axis=-1, keepdims=True) + 1e-6)
    f = jnp.where(c < GDN_HEADS, rs * (GDN_DK ** -0.5), jnp.where(c < 2 * GDN_HEADS, rs, 1.0))
    o_ref[0] = (y * f).astype(o_ref.dtype)


def conv_prep(proj_a, conv_w, *, ts=2048):
    b, s, _ = proj_a.shape
    ts = _tile(s, ts)
    r = ts // SUBLANES
    last = s // SUBLANES - 1
    return pl.pallas_call(
        _conv_kernel,
        grid=(b, s // ts, GDN_QKV // LANES),
        in_specs=[pl.BlockSpec((1, ts, LANES), lambda b_, i, c: (b_, i, c)),
                  pl.BlockSpec((1, SUBLANES, LANES), lambda b_, i, c: (b_, jnp.maximum(i * r - 1, 0), c)),
                  pl.BlockSpec((1, SUBLANES, LANES), lambda b_, i, c: (b_, jnp.minimum((i + 1) * r, last), c)),
                  pl.BlockSpec((GDN_CONV, LANES), lambda b_, i, c: (0, c))],
        out_specs=pl.BlockSpec((1, ts, LANES), lambda b_, i, c: (b_, i, c)),
        out_shape=jax.ShapeDtypeStruct((b, s, GDN_QKV), BF16),
        scratch_shapes=[pltpu.VMEM((ts + 2 * SUBLANES, LANES), F32)],
        compiler_params=_params("parallel", "parallel", "parallel"),
        name="gdn_conv_prep",
    )(proj_a, proj_a, proj_a, conv_w)


def _split3(x):
    hi = x.astype(BF16)
    r = x - hi.astype(F32)
    mid = r.astype(BF16)
    lo = (r - mid.astype(F32)).astype(BF16)
    return hi, mid, lo


def _gdn_kernel(q_ref, k_ref, v_ref, gc_ref, bc_ref, gr_ref, o_ref, state_ref):
    d, i = pl.program_id(1), pl.program_id(2)
    c = GDN_CHUNK

    @pl.when(i == 0)
    def _():
        state_ref[...] = jnp.zeros_like(state_ref)

    row = lax.broadcasted_iota(jnp.int32, (c, c), 0)
    col = lax.broadcasted_iota(jnp.int32, (c, c), 1)
    order = jnp.where(d == 0, row - col, col - row)
    incl = order >= 0
    strict = order > 0
    eye = jnp.where(row == col, 1.0, 0.0)
    ones_incl = jnp.where(incl, 1.0, 0.0).astype(BF16)

    g_col = gc_ref[0, 0]
    g_row = gr_ref[0, 0]
    gcum_col = sum(_dot(ones_incl, p) for p in _split3(g_col))
    gcum_row = sum(_dot_nt(p, ones_incl) for p in _split3(g_row))
    g_tot = jnp.sum(g_col, axis=0, keepdims=True)
    beta_all = bc_ref[0, 0]

    heads = range(GDN_HEADS)
    hs = [slice(h * GDN_DK, (h + 1) * GDN_DK) for h in heads]
    q = [q_ref[0, :, hs[h]] for h in heads]
    k = [k_ref[0, :, hs[h]] for h in heads]
    beta = [beta_all[:, h:h + 1] for h in heads]
    gcc = [gcum_col[:, h:h + 1] for h in heads]
    gt = [g_tot[:, h:h + 1] for h in heads]
    kf = [k[h].astype(F32) for h in heads]
    kb = [kf[h] * beta[h] for h in heads]
    akk = [_dot_nt(kb[h].astype(BF16), k[h]) for h in heads]
    qk = [_dot_nt(q[h], k[h]) for h in heads]
    decay = [jnp.where(incl, jnp.exp(jnp.where(incl, gcc[h] - gcum_row[h:h + 1, :], 0.0)), 0.0)
             for h in heads]

    p = [-jnp.where(strict, akk[h] * decay[h], 0.0) for h in heads]
    t_inv = [eye + p[h] for h in heads]
    for _ in range(int(math.log2(c)) - 1):
        pb = [p[h].astype(BF16) for h in heads]
        p = [_dot(pb[h], pb[h]) for h in heads]
        t_inv = [t_inv[h] + _dot(t_inv[h].astype(BF16), p[h].astype(BF16)) for h in heads]

    eg = [jnp.exp(gcc[h]) for h in heads]
    rhs = [jnp.concatenate([(v_ref[0, :, hs[h]].astype(F32) * beta[h]).astype(BF16),
                            (kb[h] * eg[h]).astype(BF16)], axis=1) for h in heads]
    uw = [_dot(t_inv[h].astype(BF16), rhs[h]) for h in heads]
    lhs_o = [jnp.concatenate([(q[h].astype(F32) * eg[h]).astype(BF16),
                              jnp.where(incl, qk[h] * decay[h], 0.0).astype(BF16)], axis=1) for h in heads]
    k_tail = [(kf[h] * jnp.exp(gt[h] - gcc[h])).astype(BF16) for h in heads]
    s = [state_ref[h] for h in heads]
    sb = [s[h].astype(BF16) for h in heads]
    vb = [(uw[h][:, :GDN_DK] - _dot(uw[h][:, GDN_DK:].astype(BF16), sb[h])).astype(BF16) for h in heads]
    o = [_dot(lhs_o[h], jnp.concatenate([sb[h], vb[h]], axis=0)) for h in heads]
    s_new = [s[h] * jnp.exp(gt[h]) + _dot_tn(k_tail[h], vb[h]) for h in heads]
    for h in heads:
        state_ref[h] = s_new[h]
        o_ref[0, 0, :, hs[h]] = o[h]


def gdn_scan(qkv_n, g_col, beta_col, g_row):
    b, s, _ = qkv_n.shape
    c = GDN_CHUNK
    nc = s // c

    def cidx(d, i):
        return i + d * (nc - 1 - 2 * i)

    def qkv_spec(part):
        return pl.BlockSpec((1, c, GDN_QK), lambda b_, d, i: (b_, cidx(d, i), part))

    col_spec = pl.BlockSpec((1, 1, c, GDN_HEADS), lambda b_, d, i: (b_, d, cidx(d, i), 0))
    row_spec = pl.BlockSpec((1, 1, GDN_HEADS, c), lambda b_, d, i: (b_, d, 0, cidx(d, i)))
    return pl.pallas_call(
        _gdn_kernel,
        grid=(b, 2, nc),
        in_specs=[qkv_spec(0), qkv_spec(1), qkv_spec(2), col_spec, col_spec, row_spec],
        out_specs=pl.BlockSpec((1, 1, c, GDN_WIDTH), lambda b_, d, i: (d, b_, cidx(d, i), 0)),
        out_shape=jax.ShapeDtypeStruct((2, b, s, GDN_WIDTH), F32),
        scratch_shapes=[pltpu.VMEM((GDN_HEADS, GDN_DK, GDN_DK), F32)],
        compiler_params=_params("parallel", "parallel", "arbitrary"),
        name="gdn_scan",
    )(qkv_n, qkv_n, qkv_n, g_col, beta_col, g_row)


def _attn_kernel(slopes_ref, q_ref, k_ref, v_ref, lam_ref, gain_ref, o_ref, m_ref, l_ref, acc_ref, tile_ref):
    h, qi, sweep, ki = pl.program_id(1), pl.program_id(2), pl.program_id(3), pl.program_id(4)
    tq, tk = q_ref.shape[1], k_ref.shape[1]
    groups = tk // LANES
    c2 = slopes_ref[h] * LOG2E * jnp.ones((1, LANES), F32)

    @pl.when((sweep == 0) & (ki == 0))
    def _():
        m_ref[...] = jnp.full_like(m_ref, NEG_INF)
        rel = (lax.broadcasted_iota(jnp.int32, (tq, tk), 1)
               - lax.broadcasted_iota(jnp.int32, (tq, tk), 0)).astype(F32)
        t0 = c2[:, 0:1] * rel
        tile_ref[0] = t0
        tile_ref[1] = -t0
        tile_ref[2] = -jnp.abs(t0)

    @pl.when((sweep == 1) & (ki == 0))
    def _():
        l_ref[...] = jnp.zeros_like(l_ref)
        acc_ref[...] = jnp.zeros_like(acc_ref)
        for c in range(2):
            m_ref[c] = jnp.broadcast_to(jnp.max(m_ref[c], axis=-1, keepdims=True), (tq, LANES))

    case = jnp.where(ki < qi, 0, jnp.where(ki > qi, 1, 2))
    gap = jnp.abs(qi * tq - ki * tk) * jnp.ones((1, LANES), jnp.int32)
    cb = -c2 * gap.astype(F32)

    def scores(c):
        cs = slice(c * DIFF_HD, (c + 1) * DIFF_HD)
        return _dot_nt(q_ref[0, :, cs], k_ref[0, :, cs]) + tile_ref[case]

    def lane_groups(x):
        return [x[:, g * LANES:(g + 1) * LANES] for g in range(groups)]

    @pl.when(sweep == 0)
    def _():
        for c in range(2):
            m_ref[c] = jnp.maximum(m_ref[c], functools.reduce(jnp.maximum, lane_groups(scores(c))) + cb)

    @pl.when(sweep == 1)
    def _():
        v = v_ref[0]
        for c in range(2):
            shift = jnp.concatenate([cb - m_ref[c]] * groups, axis=1)
            p = jnp.exp2(scores(c) + shift)
            l_ref[c] += functools.reduce(jnp.add, lane_groups(p))
            acc_ref[c] += _dot(p.astype(BF16), v)

    @pl.when((sweep == 1) & (ki == pl.num_programs(4) - 1))
    def _():
        lp = lam_ref[...]
        lam = (jnp.exp(jnp.sum(lp[0:1] * lp[1:2], axis=-1, keepdims=True))
               - jnp.exp(jnp.sum(lp[2:3] * lp[3:4], axis=-1, keepdims=True)) + LAM_INIT)
        l0 = jnp.sum(l_ref[0], axis=-1, keepdims=True)
        l1 = jnp.sum(l_ref[1], axis=-1, keepdims=True)
        o = acc_ref[0] / l0 - lam * (acc_ref[1] / l1)
        o_ref[0] = (_rms(o, gain_ref[...]) * (1.0 - LAM_INIT)).astype(o_ref.dtype)


def diff_attention(proj_b, slopes, diff_lambda, gain, *, tile=512):
    b, s, _ = proj_b.shape
    tq = tk = _tile(s, tile)
    e = 2 * DIFF_HD
    return pl.pallas_call(
        _attn_kernel,
        grid=(b, DIFF_HEADS, s // tq, 2, s // tk),
        in_specs=[pl.BlockSpec(memory_space=pltpu.SMEM),
                  pl.BlockSpec((1, tq, e), lambda b_, h, qi, sw, ki: (b_, qi, h)),
                  pl.BlockSpec((1, tk, e), lambda b_, h, qi, sw, ki: (b_, ki, DIFF_HEADS + h)),
                  pl.BlockSpec((1, tk, e), lambda b_, h, qi, sw, ki: (b_, ki * sw, 2 * DIFF_HEADS + h)),
                  pl.BlockSpec((4, DIFF_HD), lambda b_, h, qi, sw, ki: (0, 0)),
                  pl.BlockSpec((1, e), lambda b_, h, qi, sw, ki: (0, 0))],
        out_specs=pl.BlockSpec((1, tq, e), lambda b_, h, qi, sw, ki: (b_, qi, h)),
        out_shape=jax.ShapeDtypeStruct((b, s, DIFF_WIDTH), BF16),
        scratch_shapes=[pltpu.VMEM((2, tq, LANES), F32),
                        pltpu.VMEM((2, tq, LANES), F32),
                        pltpu.VMEM((2, tq, e), F32),
                        pltpu.VMEM((3, tq, tk), F32)],
        compiler_params=_params("parallel", "parallel", "parallel", "arbitrary", "arbitrary"),
        name="diff_attention",
    )(slopes, proj_b, proj_b, proj_b, diff_lambda, gain.reshape(1, e))


def _merge_kernel(o_ref, z_ref, ga_ref, gb_ref, ob_ref, wa_ref, wb_ref, gain_ref, out_ref, oa_ref):
    @pl.when(pl.program_id(1) == 0)
    def _():
        gain = gain_ref[...]
        for h in range(GDN_HEADS):
            hs = slice(h * GDN_DK, (h + 1) * GDN_DK)
            z = z_ref[:, hs]
            oa_ref[:, hs] = (_rms(o_ref[0, :, hs] + o_ref[1, :, hs], gain) * (z * _sigmoid(z))).astype(BF16)

    out_ref[...] = (_sigmoid(ga_ref[...]) * _dot(oa_ref[...], wa_ref[...])
                    + _sigmoid(gb_ref[...]) * _dot(ob_ref[...], wb_ref[...])).astype(out_ref.dtype)


def merge_branches(o_gdn, proj_a, o_b, w_a, w_b, gain, *, tm=512, tn=1024):
    t = o_b.shape[0]
    tm = _tile(t, tm)
    z_blk = GDN_QKV // GDN_WIDTH
    ga_blk = (GDN_QKV + GDN_WIDTH) // tn
    gb_blk = (GDN_QKV + GDN_WIDTH + D_MODEL) // tn
    return pl.pallas_call(
        _merge_kernel,
        grid=(t // tm, D_MODEL // tn),
        in_specs=[pl.BlockSpec((2, tm, GDN_WIDTH), lambda i, j: (0, i, 0)),
                  pl.BlockSpec((tm, GDN_WIDTH), lambda i, j: (i, z_blk)),
                  pl.BlockSpec((tm, tn), lambda i, j: (i, ga_blk + j)),
                  pl.BlockSpec((tm, tn), lambda i, j: (i, gb_blk + j)),
                  pl.BlockSpec((tm, DIFF_WIDTH), lambda i, j: (i, 0)),
                  pl.BlockSpec((GDN_WIDTH, tn), lambda i, j: (0, j)),
                  pl.BlockSpec((DIFF_WIDTH, tn), lambda i, j: (0, j)),
                  pl.BlockSpec((1, GDN_DK), lambda i, j: (0, 0))],
        out_specs=pl.BlockSpec((tm, tn), lambda i, j: (i, j)),
        out_shape=jax.ShapeDtypeStruct((t, D_MODEL), BF16),
        scratch_shapes=[pltpu.VMEM((tm, GDN_WIDTH), BF16)],
        compiler_params=_params("parallel", "arbitrary"),
        name="merge_branches",
    )(o_gdn, proj_a, proj_a, proj_a, o_b, w_a, w_b, gain.reshape(1, GDN_DK))


def _extract_distinct(x, n):
    vals, cnts = [], []
    for _ in range(n):
        m = jnp.max(x, axis=0, keepdims=True)
        eq = x == m
        cnt = jnp.sum(jnp.where(eq, 1.0, 0.0), axis=0, keepdims=True)
        vals.append(m)
        cnts.append(jnp.where(m == NEG_INF, 0.0, cnt))
        x = jnp.where(eq, NEG_INF, x)
    return vals, cnts


def _pair_sums(at, bt):
    s8 = SUBLANES
    return jnp.concatenate([at[k:k + 1] + bt[0:s8] for k in range(s8)]
                           + [at[0:1] + bt[s8:], at[s8:] + bt[0:1]], axis=0)


def _kth_largest(x, mult, k):
    tau = jnp.zeros_like(x[0:1])
    cum = jnp.zeros_like(x[0:1])
    for _ in range(k):
        m = jnp.max(x, axis=0, keepdims=True)
        eq = x == m
        tau = jnp.where(cum < k, m, tau)
        cum = cum + jnp.sum(jnp.where(eq, mult, 0.0), axis=0, keepdims=True)
        x = jnp.where(eq, NEG_INF, x)
    return tau


def _route_kernel(q_ref, keys_ref, a_ref, b_ref, rows_ref):
    half = PEER_DQ // 2
    s1 = _dot_nt(keys_ref[0, 0], q_ref[:, 0:half])
    s2 = _dot_nt(keys_ref[0, 1], q_ref[:, half:PEER_DQ])
    v1, n1 = _extract_distinct(s1, PEER_TOPK)
    v2, n2 = _extract_distinct(s2, PEER_TOPK)
    at = jnp.concatenate([(v - v1[0]) * LOG2E for v in v1], axis=0)
    bt = jnp.concatenate([(v - v2[0]) * LOG2E for v in v2], axis=0)
    s8 = SUBLANES
    na = jnp.concatenate(n1, axis=0)
    nb = jnp.concatenate(n2, axis=0)
    mult = jnp.concatenate([na[k:k + 1] * nb[0:s8] for k in range(s8)]
                           + [na[0:1] * nb[s8:], na[s8:] * nb[0:1]], axis=0)
    cand = _pair_sums(at, bt)
    tau = _kth_largest(cand, mult, PEER_TOPK)
    z = jnp.sum(jnp.where(cand >= tau, mult * jnp.exp2(cand), 0.0), axis=0, keepdims=True)
    nlz = -jnp.log2(z)
    a_ref[0] = (s1 - v1[0]) * LOG2E
    b_ref[0] = (s2 - v2[0]) * LOG2E + nlz
    tau_z = _kth_largest(_pair_sums(at, bt + nlz), mult, PEER_TOPK)
    rows_ref[0] = jnp.concatenate([tau_z, jnp.zeros((s8 - 1, tau_z.shape[1]), F32)], axis=0)


def peer_route(q, keys, *, tt=256):
    t = q.shape[0]
    tt = _tile(t, tt)
    tab = jax.ShapeDtypeStruct((PEER_HEADS, PEER_NKEYS, t), F32)
    tab_spec = pl.BlockSpec((1, PEER_NKEYS, tt), lambda i, p: (p, 0, i))
    return pl.pallas_call(
        _route_kernel,
        grid=(t // tt, PEER_HEADS),
        in_specs=[pl.BlockSpec((tt, PEER_DQ), lambda i, p: (i, p)),
                  pl.BlockSpec((1, 2, PEER_NKEYS, PEER_DQ // 2), lambda i, p: (p, 0, 0, 0))],
        out_specs=[tab_spec, tab_spec, pl.BlockSpec((1, SUBLANES, tt), lambda i, p: (p, 0, i))],
        out_shape=[tab, tab, jax.ShapeDtypeStruct((PEER_HEADS, SUBLANES, t), F32)],
        compiler_params=_params("parallel", "parallel"),
        name="peer_route",
    )(q, keys)


def _gelu_tanh(x):
    return x * (0.5 * (1.0 + jnp.tanh(math.sqrt(2.0 / math.pi) * (x + 0.044715 * (x * x * x)))))


def _peer_kernel(x_ref, gf_ref, u_ref, v_ref, a_ref, b_ref, rows_ref, gfin_ref,
                 o_ref, h_ref, acc_ref, gate_ref, coef_ref):
    e = pl.program_id(1)
    te, tt = u_ref.shape[0], x_ref.shape[0]
    half = te // 2
    rows = half // PEER_NKEYS

    @pl.when(e == 0)
    def _():
        h_ref[...] = _rms(x_ref[...], gf_ref[...]).astype(BF16)
        acc_ref[...] = jnp.zeros_like(acc_ref)

    for hf in range(2):
        for il in range(rows):
            rs = slice(il * PEER_NKEYS, (il + 1) * PEER_NKEYS)
            i = e * (2 * rows) + hf * rows + il
            a_rows = [a_ref[p, pl.ds(i, 1), :] for p in range(PEER_HEADS)]
            for lb in range(tt // LANES):
                ls = slice(lb * LANES, (lb + 1) * LANES)
                g = None
                for p in range(PEER_HEADS):
                    pair = a_rows[p][:, ls] + b_ref[p, :, ls]
                    w = jnp.exp2(jnp.where(pair >= rows_ref[p, 0:1, ls], pair, NEG_INF))
                    g = w if g is None else g + w
                gate_ref[hf, rs, ls] = g
        act = _gelu_tanh(_dot_nt(u_ref[hf * half:(hf + 1) * half, :], h_ref[...]))
        coef_ref[hf] = (act * gate_ref[hf]).astype(BF16)
        acc_ref[...] += _dot_tn(coef_ref[hf], v_ref[hf * half:(hf + 1) * half, :])

    @pl.when(e == pl.num_programs(1) - 1)
    def _():
        o_ref[...] = _rms(x_ref[...] + acc_ref[...], gfin_ref[...])


def peer_experts(x1, gain_ffn, u, v, a, b, rows, gain_final, *, tt=512, te=1024):
    t, d = x1.shape
    tt, te = _tile(t, tt), _tile(PEER_EXPERTS, te)
    tab_spec = pl.BlockSpec((PEER_HEADS, PEER_NKEYS, tt), lambda i, e: (0, 0, i))
    return pl.pallas_call(
        _peer_kernel,
        grid=(t // tt, PEER_EXPERTS // te),
        in_specs=[pl.BlockSpec((tt, d), lambda i, e: (i, 0), pipeline_mode=pl.Buffered(1)),
                  pl.BlockSpec((1, d), lambda i, e: (0, 0)),
                  pl.BlockSpec((te, d), lambda i, e: (e, 0)),
                  pl.BlockSpec((te, d), lambda i, e: (e, 0)),
                  tab_spec, tab_spec,
                  pl.BlockSpec((PEER_HEADS, SUBLANES, tt), lambda i, e: (0, 0, i)),
                  pl.BlockSpec((1, d), lambda i, e: (0, 0))],
        out_specs=pl.BlockSpec((tt, d), lambda i, e: (i, 0)),
        out_shape=jax.ShapeDtypeStruct((t, d), F32),
        scratch_shapes=[pltpu.VMEM((tt, d), BF16),
                        pltpu.VMEM((tt, d), F32),
                        pltpu.VMEM((2, te // 2, tt), F32),
                        pltpu.VMEM((2, te // 2, tt), BF16)],
        compiler_params=_params("parallel", "arbitrary"),
        name="peer_experts",
    )(x1, gain_ffn.reshape(1, d), u, v, a, b, rows, gain_final.reshape(1, d))


def _prepare_weights(w_in, gdn_a_log, gdn_dt_bias, w_branch_a, w_branch_b, w_out, peer_w_q,
                     peer_sub_keys, peer_u, peer_v):
    o_z = GDN_QKV + GDN_WIDTH
    o_ab = o_z + 4 * GDN_HEADS
    o_b = o_ab + 2 * DIFF_QK + DIFF_WIDTH
    w = w_in[0]
    pad = jnp.zeros((D_MODEL, LANES - 4 * GDN_HEADS), F32)
    row = lambda p: jnp.concatenate([p[0].reshape(1, -1), jnp.zeros((1, LANES - 2 * GDN_HEADS), F32)], axis=1)
    return dict(
        w_a=jnp.concatenate([w[:, :o_z], w[:, o_b:]], axis=1).astype(BF16),
        w_ab=jnp.concatenate([w[:, o_z:o_ab], pad], axis=1).astype(BF16),
        w_b=w[:, o_ab:o_b].astype(BF16),
        alog=row(gdn_a_log), dtb=row(gdn_dt_bias),
        qscale=jnp.concatenate([jnp.full((1, DIFF_QK), DIFF_HD ** -0.5 * LOG2E, F32),
                                jnp.ones((1, DIFF_QK + DIFF_WIDTH), F32)], axis=1),
        w_branch_a=w_branch_a[0].astype(BF16), w_branch_b=w_branch_b[0].astype(BF16),
        w_out=w_out[0].astype(BF16), w_q=peer_w_q[0].astype(BF16),
        keys=peer_sub_keys[0].astype(BF16), u=peer_u[0].astype(BF16), v=peer_v[0].astype(BF16),
    )


def _trunk(x, pw, norm_mix_gain, conv_w, gdn_norm_gain, diff_lambda, diff_norm_gain,
           norm_ffn_gain, norm_final_gain, slopes):
    b, s, d = x.shape
    t = b * s
    xt = x.reshape(t, d)
    g_mix = norm_mix_gain[0]

    proj_a = norm_matmul(xt, g_mix, pw["w_a"], F32, name="in_proj_a")
    proj_b = norm_matmul(xt, g_mix, pw["w_b"], BF16, epilogue=_scale_epilogue,
                         extra=(pw["qscale"],), name="in_proj_b")
    gb = norm_matmul(xt, g_mix, pw["w_ab"], F32, epilogue=_gate_epilogue,
                     extra=(pw["alog"], pw["dtb"]), name="in_proj_gates")

    qkv_n = conv_prep(proj_a.reshape(b, s, -1), conv_w[0])
    g = gb[:, :2 * GDN_HEADS].reshape(b, s, 2, GDN_HEADS)
    beta = gb[:, 2 * GDN_HEADS:4 * GDN_HEADS].reshape(b, s, 2, GDN_HEADS)
    g_col = jnp.transpose(g, (0, 2, 1, 3))
    beta_col = jnp.transpose(beta, (0, 2, 1, 3))
    g_row = jnp.transpose(g, (0, 2, 3, 1))
    o_gdn = gdn_scan(qkv_n, g_col, beta_col, g_row).reshape(2, t, GDN_WIDTH)

    o_b = diff_attention(proj_b.reshape(b, s, -1), slopes, diff_lambda[0], diff_norm_gain[0])
    o_b = o_b.reshape(t, DIFF_WIDTH)

    merged = merge_branches(o_gdn, proj_a, o_b, pw["w_branch_a"], pw["w_branch_b"], gdn_norm_gain[0])
    x1 = matmul_residual(merged, pw["w_out"], xt, name="out_proj")

    q = norm_matmul(x1, norm_ffn_gain[0], pw["w_q"], BF16, name="peer_query")
    a, bb, rows = peer_route(q, pw["keys"])
    y = peer_experts(x1, norm_ffn_gain[0], pw["u"], pw["v"], a, bb, rows, norm_final_gain)
    return y.reshape(b, s, d)


def kernel(x_prompt, x_sample, norm_mix_gain, w_in, conv_w, gdn_a_log, gdn_dt_bias, gdn_norm_gain,
           diff_lambda, diff_norm_gain, w_branch_a, w_branch_b, w_out, norm_ffn_gain,
           peer_w_q, peer_sub_keys, peer_u, peer_v, norm_final_gain):
    pw = _prepare_weights(w_in, gdn_a_log, gdn_dt_bias, w_branch_a, w_branch_b, w_out, peer_w_q,
                          peer_sub_keys, peer_u, peer_v)
    slopes = 2.0 ** (-8.0 * jnp.arange(1, DIFF_HEADS + 1, dtype=F32) / DIFF_HEADS)
    run = functools.partial(_trunk, pw=pw, norm_mix_gain=norm_mix_gain, conv_w=conv_w,
                            gdn_norm_gain=gdn_norm_gain, diff_lambda=diff_lambda,
                            diff_norm_gain=diff_norm_gain, norm_ffn_gain=norm_ffn_gain,
                            norm_final_gain=norm_final_gain, slopes=slopes)
    return (run(x_prompt), run(x_sample))
```

```python
import functools
import math

import jax
import jax.numpy as jnp
from jax import lax
from jax.experimental import pallas as pl
from jax.experimental.pallas import tpu as pltpu

F32 = jnp.float32
BF16 = jnp.bfloat16

D_MODEL = 2048
GDN_HEADS = 8
GDN_DK = 128
GDN_QK = 1024
GDN_WIDTH = 1024
GDN_QKV = 3072
GDN_CONV = 5
GDN_CHUNK = 128
DIFF_HEADS = 4
DIFF_HD = 128
DIFF_QK = 1024
DIFF_WIDTH = 1024
PEER_HEADS = 8
PEER_NKEYS = 128
PEER_EXPERTS = PEER_NKEYS * PEER_NKEYS
PEER_DQ = 256
PEER_TOPK = 16
NORM_EPS = 1e-6
LAM_INIT = 0.8 - 0.6 * math.exp(-0.3 * 0)

LANES = 128
SUBLANES = 8
VMEM_LIMIT_BYTES = 56 * 1024 * 1024

NEG_INF = float("-inf")
LOG2E = math.log2(math.e)


def _params(*sem):
    return pltpu.CompilerParams(dimension_semantics=sem, vmem_limit_bytes=VMEM_LIMIT_BYTES)


def _tile(n, pref):
    t = min(n, pref)
    while n % t:
        t //= 2
    return t


def _dot(a, b):
    return jnp.dot(a, b, preferred_element_type=F32)


def _dot_nt(a, b):
    return lax.dot_general(a, b, (((1,), (1,)), ((), ())), preferred_element_type=F32)


def _dot_tn(a, b):
    return lax.dot_general(a, b, (((0,), (0,)), ((), ())), preferred_element_type=F32)


def _sigmoid(x):
    return 1.0 / (1.0 + jnp.exp(-x))


def _rms(x, gain):
    return x * lax.rsqrt(jnp.mean(x * x, axis=-1, keepdims=True) + NORM_EPS) * gain


def _norm_matmul_kernel(x_ref, g_ref, w_ref, *rest, epilogue, n_extra):
    extra, o_ref, h_ref = rest[:n_extra], rest[n_extra], rest[n_extra + 1]

    @pl.when(pl.program_id(1) == 0)
    def _():
        h_ref[...] = _rms(x_ref[...], g_ref[...]).astype(BF16)

    acc = _dot(h_ref[...], w_ref[...])
    if epilogue is not None:
        acc = epilogue(acc, *[e[...] for e in extra])
    if len(o_ref.shape) == 2:
        o_ref[...] = acc.astype(o_ref.dtype)
    else:
        gw = o_ref.shape[2]
        for gi in range(o_ref.shape[0]):
            o_ref[gi] = acc[:, gi * gw:(gi + 1) * gw].astype(o_ref.dtype)


def norm_matmul(x, gain, w, out_dtype, *, epilogue=None, extra=(), tm=1024, tn=1024, group=None, name):
    t, d = x.shape
    n = w.shape[1]
    tm, tn = _tile(t, tm), _tile(n, tn)
    kern = functools.partial(_norm_matmul_kernel, epilogue=epilogue, n_extra=len(extra))
    if group is None:
        out_spec = pl.BlockSpec((tm, tn), lambda i, j: (i, j))
        out_shape = jax.ShapeDtypeStruct((t, n), out_dtype)
    else:
        out_spec = pl.BlockSpec((tn // group, tm, group), lambda i, j: (j, i, 0))
        out_shape = jax.ShapeDtypeStruct((n // group, t, group), out_dtype)
    return pl.pallas_call(
        kern,
        grid=(t // tm, n // tn),
        in_specs=[pl.BlockSpec((tm, d), lambda i, j: (i, 0)),
                  pl.BlockSpec((1, d), lambda i, j: (0, 0)),
                  pl.BlockSpec((d, tn), lambda i, j: (0, j))]
                 + [pl.BlockSpec((1, tn), lambda i, j: (0, j)) for _ in extra],
        out_specs=out_spec,
        out_shape=out_shape,
        scratch_shapes=[pltpu.VMEM((tm, d), BF16)],
        compiler_params=_params("parallel", "arbitrary"),
        name=name,
    )(x, gain.reshape(1, d), w, *extra)


def _gate_epilogue(acc, alog, dtb):
    z = acc + dtb
    softplus = jnp.maximum(z, 0.0) + jnp.log(1.0 + jnp.exp(-jnp.abs(z)))
    g = -jnp.exp(alog) * softplus
    lane = lax.broadcasted_iota(jnp.int32, acc.shape, 1)
    return jnp.where(lane < 2 * GDN_HEADS, g, _sigmoid(acc))


def _scale_epilogue(acc, scale):
    return acc * scale


def _matmul_res_kernel(a_ref, w_ref, r_ref, o_ref):
    o_ref[...] = r_ref[...] + _dot(a_ref[...], w_ref[...])


def matmul_residual(a, w, res, *, tm=1024, tn=1024, name):
    t, k = a.shape
    n = w.shape[1]
    tm, tn = _tile(t, tm), _tile(n, tn)
    return pl.pallas_call(
        _matmul_res_kernel,
        grid=(t // tm, n // tn),
        in_specs=[pl.BlockSpec((tm, k), lambda i, j: (i, 0)),
                  pl.BlockSpec((k, tn), lambda i, j: (0, j)),
                  pl.BlockSpec((tm, tn), lambda i, j: (i, j))],
        out_specs=pl.BlockSpec((tm, tn), lambda i, j: (i, j)),
        out_shape=jax.ShapeDtypeStruct((t, n), F32),
        compiler_params=_params("parallel", "arbitrary"),
        name=name,
    )(a, w, res)


def _conv_kernel(cur_ref, prev_ref, next_ref, w_ref, o_ref, pad_ref):
    i, c = pl.program_id(1), pl.program_id(2)
    ts = cur_ref.shape[1]
    halo = SUBLANES
    pad_ref[0:halo, :] = jnp.where(i > 0, prev_ref[0], 0.0)
    pad_ref[halo:halo + ts, :] = cur_ref[0]
    pad_ref[halo + ts:2 * halo + ts, :] = jnp.where(i < pl.num_programs(1) - 1, next_ref[0], 0.0)
    w = w_ref[...]
    first = halo - (GDN_CONV - 1) // 2
    acc = w[0:1, :] * pad_ref[first:first + ts, :]
    for k in range(1, GDN_CONV):
        acc = acc + w[k:k + 1, :] * pad_ref[first + k:first + k + ts, :]
    y = acc * _sigmoid(acc)
    rs = lax.rsqrt(jnp.sum(y * y, axis=-1, keepdims=True) + 1e-6)
    f = jnp.where(c < GDN_HEADS, rs * (GDN_DK ** -0.5), jnp.where(c < 2 * GDN_HEADS, rs, 1.0))
    o_ref[0] = (y * f).astype(o_ref.dtype)


def conv_prep(proj_a, conv_w, *, ts=2048):
    b, s, _ = proj_a.shape
    ts = _tile(s, ts)
    r = ts // SUBLANES
    last = s // SUBLANES - 1
    return pl.pallas_call(
        _conv_kernel,
        grid=(b, s // ts, GDN_QKV // LANES),
        in_specs=[pl.BlockSpec((1, ts, LANES), lambda b_, i, c: (b_, i, c)),
                  pl.BlockSpec((1, SUBLANES, LANES), lambda b_, i, c: (b_, jnp.maximum(i * r - 1, 0), c)),
                  pl.BlockSpec((1, SUBLANES, LANES), lambda b_, i, c: (b_, jnp.minimum((i + 1) * r, last), c)),
                  pl.BlockSpec((GDN_CONV, LANES), lambda b_, i, c: (0, c))],
        out_specs=pl.BlockSpec((1, ts, LANES), lambda b_, i, c: (b_, i, c)),
        out_shape=jax.ShapeDtypeStruct((b, s, GDN_QKV), BF16),
        scratch_shapes=[pltpu.VMEM((ts + 2 * SUBLANES, LANES), F32)],
        compiler_params=_params("parallel", "parallel", "parallel"),
        name="gdn_conv_prep",
    )(proj_a, proj_a, proj_a, conv_w)


def _split3(x):
    hi = x.astype(BF16)
    r = x - hi.astype(F32)
    mid = r.astype(BF16)
    lo = (r - mid.astype(F32)).astype(BF16)
    return hi, mid, lo


def _gdn_kernel(q_ref, k_ref, v_ref, gb_ref, gr_ref, o_ref, state_ref):
    d, i = pl.program_id(1), pl.program_id(2)
    c = GDN_CHUNK

    @pl.when(i == 0)
    def _():
        state_ref[...] = jnp.zeros_like(state_ref)

    row = lax.broadcasted_iota(jnp.int32, (c, c), 0)
    col = lax.broadcasted_iota(jnp.int32, (c, c), 1)
    order = jnp.where(d == 0, row - col, col - row)
    incl = order >= 0
    strict = order > 0
    eye = jnp.where(row == col, 1.0, 0.0)
    ones_incl = jnp.where(incl, 1.0, 0.0).astype(BF16)

    gb = gb_ref[0, 0]
    g_col = gb[:, 0:GDN_HEADS]
    g_row = gr_ref[0, 0]
    gcum_col = sum(_dot(ones_incl, p) for p in _split3(g_col))
    gcum_row = sum(_dot_nt(p, ones_incl) for p in _split3(g_row))
    g_tot = jnp.sum(g_col, axis=0, keepdims=True)
    beta_all = gb[:, GDN_HEADS:2 * GDN_HEADS]

    heads = range(GDN_HEADS)
    hs = [slice(h * GDN_DK, (h + 1) * GDN_DK) for h in heads]
    q = [q_ref[0, :, hs[h]] for h in heads]
    k = [k_ref[0, :, hs[h]] for h in heads]
    beta = [beta_all[:, h:h + 1] for h in heads]
    gcc = [gcum_col[:, h:h + 1] for h in heads]
    gt = [g_tot[:, h:h + 1] for h in heads]
    kf = [k[h].astype(F32) for h in heads]
    kb = [kf[h] * beta[h] for h in heads]
    akk = [_dot_nt(kb[h].astype(BF16), k[h]) for h in heads]
    qk = [_dot_nt(q[h], k[h]) for h in heads]
    decay = [jnp.where(incl, jnp.exp(jnp.where(incl, gcc[h] - gcum_row[h:h + 1, :], 0.0)), 0.0)
             for h in heads]

    p = [-jnp.where(strict, akk[h] * decay[h], 0.0) for h in heads]
    t_inv = [eye + p[h] for h in heads]
    for _ in range(int(math.log2(c)) - 1):
        pb = [p[h].astype(BF16) for h in heads]
        p = [_dot(pb[h], pb[h]) for h in heads]
        t_inv = [t_inv[h] + _dot(t_inv[h].astype(BF16), p[h].astype(BF16)) for h in heads]

    eg = [jnp.exp(gcc[h]) for h in heads]
    rhs = [jnp.concatenate([(v_ref[0, :, hs[h]].astype(F32) * beta[h]).astype(BF16),
                            (kb[h] * eg[h]).astype(BF16)], axis=1) for h in heads]
    uw = [_dot(t_inv[h].astype(BF16), rhs[h]) for h in heads]
    lhs_o = [jnp.concatenate([(q[h].astype(F32) * eg[h]).astype(BF16),
                              jnp.where(incl, qk[h] * decay[h], 0.0).astype(BF16)], axis=1) for h in heads]
    k_tail = [(kf[h] * jnp.exp(gt[h] - gcc[h])).astype(BF16) for h in heads]
    s = [state_ref[h] for h in heads]
    sb = [s[h].astype(BF16) for h in heads]
    vb = [(uw[h][:, :GDN_DK] - _dot(uw[h][:, GDN_DK:].astype(BF16), sb[h])).astype(BF16) for h in heads]
    o = [_dot(lhs_o[h], jnp.concatenate([sb[h], vb[h]], axis=0)) for h in heads]
    s_new = [s[h] * jnp.exp(gt[h]) + _dot_tn(k_tail[h], vb[h]) for h in heads]
    for h in heads:
        state_ref[h] = s_new[h]
        o_ref[0, 0, :, hs[h]] = o[h]


def gdn_scan(qkv_n, gb_col, g_row):
    b, s, _ = qkv_n.shape
    c = GDN_CHUNK
    nc = s // c

    def cidx(d, i):
        return i + d * (nc - 1 - 2 * i)

    def qkv_spec(part):
        return pl.BlockSpec((1, c, GDN_QK), lambda b_, d, i: (b_, cidx(d, i), part))

    col_spec = pl.BlockSpec((1, 1, c, LANES), lambda b_, d, i: (b_, d, cidx(d, i), 0))
    row_spec = pl.BlockSpec((1, 1, GDN_HEADS, c), lambda b_, d, i: (b_, d, 0, cidx(d, i)))
    return pl.pallas_call(
        _gdn_kernel,
        grid=(b, 2, nc),
        in_specs=[qkv_spec(0), qkv_spec(1), qkv_spec(2), col_spec, row_spec],
        out_specs=pl.BlockSpec((1, 1, c, GDN_WIDTH), lambda b_, d, i: (d, b_, cidx(d, i), 0)),
        out_shape=jax.ShapeDtypeStruct((2, b, s, GDN_WIDTH), F32),
        scratch_shapes=[pltpu.VMEM((GDN_HEADS, GDN_DK, GDN_DK), F32)],
        compiler_params=_params("parallel", "parallel", "arbitrary"),
        name="gdn_scan",
    )(qkv_n, qkv_n, qkv_n, gb_col, g_row)


def _attn_kernel(slopes_ref, q_ref, k_ref, v_ref, lam_ref, gain_ref, o_ref, m_ref, l_ref, acc_ref, tile_ref):
    h, qi, sweep, ki = pl.program_id(1), pl.program_id(2), pl.program_id(3), pl.program_id(4)
    tq, tk = q_ref.shape[2], k_ref.shape[2]
    groups = tk // LANES
    c2 = slopes_ref[h] * LOG2E * jnp.ones((1, LANES), F32)

    @pl.when((sweep == 0) & (ki == 0))
    def _():
        m_ref[...] = jnp.full_like(m_ref, NEG_INF)
        rel = (lax.broadcasted_iota(jnp.int32, (tq, tk), 1)
               - lax.broadcasted_iota(jnp.int32, (tq, tk), 0)).astype(F32)
        t0 = c2[:, 0:1] * rel
        tile_ref[0] = t0
        tile_ref[1] = -t0
        tile_ref[2] = -jnp.abs(t0)

    @pl.when((sweep == 1) & (ki == 0))
    def _():
        l_ref[...] = jnp.zeros_like(l_ref)
        acc_ref[...] = jnp.zeros_like(acc_ref)
        for c in range(2):
            m_ref[c] = jnp.broadcast_to(jnp.max(m_ref[c], axis=-1, keepdims=True), (tq, LANES))

    case = jnp.where(ki < qi, 0, jnp.where(ki > qi, 1, 2))
    gap = jnp.abs(qi * tq - ki * tk) * jnp.ones((1, LANES), jnp.int32)
    cb = -c2 * gap.astype(F32)

    def scores(c):
        cs = slice(c * DIFF_HD, (c + 1) * DIFF_HD)
        return _dot_nt(q_ref[0, 0, :, cs], k_ref[0, 0, :, cs]) + tile_ref[case]

    def lane_groups(x):
        return [x[:, g * LANES:(g + 1) * LANES] for g in range(groups)]

    @pl.when(sweep == 0)
    def _():
        for c in range(2):
            m_ref[c] = jnp.maximum(m_ref[c], functools.reduce(jnp.maximum, lane_groups(scores(c))) + cb)

    @pl.when(sweep == 1)
    def _():
        v = v_ref[0, 0]
        for c in range(2):
            shift = jnp.concatenate([cb - m_ref[c]] * groups, axis=1)
            p = jnp.exp2(scores(c) + shift)
            l_ref[c] += functools.reduce(jnp.add, lane_groups(p))
            acc_ref[c] += _dot(p.astype(BF16), v)

    @pl.when((sweep == 1) & (ki == pl.num_programs(4) - 1))
    def _():
        lp = lam_ref[...]
        lam = (jnp.exp(jnp.sum(lp[0:1] * lp[1:2], axis=-1, keepdims=True))
               - jnp.exp(jnp.sum(lp[2:3] * lp[3:4], axis=-1, keepdims=True)) + LAM_INIT)
        l0 = jnp.sum(l_ref[0], axis=-1, keepdims=True)
        l1 = jnp.sum(l_ref[1], axis=-1, keepdims=True)
        o = acc_ref[0] / l0 - lam * (acc_ref[1] / l1)
        o_ref[0] = (_rms(o, gain_ref[...]) * (1.0 - LAM_INIT)).astype(o_ref.dtype)


def diff_attention(proj_b, slopes, diff_lambda, gain, *, tile=512):
    _, b, s, e = proj_b.shape
    tq = tk = _tile(s, tile)
    return pl.pallas_call(
        _attn_kernel,
        grid=(b, DIFF_HEADS, s // tq, 2, s // tk),
        in_specs=[pl.BlockSpec(memory_space=pltpu.SMEM),
                  pl.BlockSpec((1, 1, tq, e), lambda b_, h, qi, sw, ki: (h, b_, qi, 0)),
                  pl.BlockSpec((1, 1, tk, e), lambda b_, h, qi, sw, ki: (DIFF_HEADS + h, b_, ki, 0)),
                  pl.BlockSpec((1, 1, tk, e), lambda b_, h, qi, sw, ki: (2 * DIFF_HEADS + h, b_, ki * sw, 0)),
                  pl.BlockSpec((4, DIFF_HD), lambda b_, h, qi, sw, ki: (0, 0)),
                  pl.BlockSpec((1, e), lambda b_, h, qi, sw, ki: (0, 0))],
        out_specs=pl.BlockSpec((1, tq, e), lambda b_, h, qi, sw, ki: (b_, qi, h)),
        out_shape=jax.ShapeDtypeStruct((b, s, DIFF_WIDTH), BF16),
        scratch_shapes=[pltpu.VMEM((2, tq, LANES), F32),
                        pltpu.VMEM((2, tq, LANES), F32),
                        pltpu.VMEM((2, tq, e), F32),
                        pltpu.VMEM((3, tq, tk), F32)],
        compiler_params=_params("parallel", "parallel", "parallel", "arbitrary", "arbitrary"),
        name="diff_attention",
    )(slopes, proj_b, proj_b, proj_b, diff_lambda, gain.reshape(1, e))


def _merge_kernel(o_ref, z_ref, ga_ref, gb_ref, ob_ref, wa_ref, wb_ref, gain_ref, out_ref, oa_ref):
    @pl.when(pl.program_id(1) == 0)
    def _():
        gain = gain_ref[...]
        for h in range(GDN_HEADS):
            hs = slice(h * GDN_DK, (h + 1) * GDN_DK)
            z = z_ref[:, hs]
            oa_ref[:, hs] = (_rms(o_ref[0, :, hs] + o_ref[1, :, hs], gain) * (z * _sigmoid(z))).astype(BF16)

    out_ref[...] = (_sigmoid(ga_ref[...]) * _dot(oa_ref[...], wa_ref[...])
                    + _sigmoid(gb_ref[...]) * _dot(ob_ref[...], wb_ref[...])).astype(out_ref.dtype)


def merge_branches(o_gdn, proj_a, o_b, w_a, w_b, gain, *, tm=512, tn=1024):
    t = o_b.shape[0]
    tm = _tile(t, tm)
    z_blk = GDN_QKV // GDN_WIDTH
    ga_blk = (GDN_QKV + GDN_WIDTH) // tn
    gb_blk = (GDN_QKV + GDN_WIDTH + D_MODEL) // tn
    return pl.pallas_call(
        _merge_kernel,
        grid=(t // tm, D_MODEL // tn),
        in_specs=[pl.BlockSpec((2, tm, GDN_WIDTH), lambda i, j: (0, i, 0)),
                  pl.BlockSpec((tm, GDN_WIDTH), lambda i, j: (i, z_blk)),
                  pl.BlockSpec((tm, tn), lambda i, j: (i, ga_blk + j)),
                  pl.BlockSpec((tm, tn), lambda i, j: (i, gb_blk + j)),
                  pl.BlockSpec((tm, DIFF_WIDTH), lambda i, j: (i, 0)),
                  pl.BlockSpec((GDN_WIDTH, tn), lambda i, j: (0, j)),
                  pl.BlockSpec((DIFF_WIDTH, tn), lambda i, j: (0, j)),
                  pl.BlockSpec((1, GDN_DK), lambda i, j: (0, 0))],
        out_specs=pl.BlockSpec((tm, tn), lambda i, j: (i, j)),
        out_shape=jax.ShapeDtypeStruct((t, D_MODEL), BF16),
        scratch_shapes=[pltpu.VMEM((tm, GDN_WIDTH), BF16)],
        compiler_params=_params("parallel", "arbitrary"),
        name="merge_branches",
    )(o_gdn, proj_a, proj_a, proj_a, o_b, w_a, w_b, gain.reshape(1, GDN_DK))


def _extract_distinct(x, n):
    vals, cnts = [], []
    for _ in range(n):
        m = jnp.max(x, axis=0, keepdims=True)
        eq = x == m
        cnt = jnp.sum(jnp.where(eq, 1.0, 0.0), axis=0, keepdims=True)
        vals.append(m)
        cnts.append(jnp.where(m == NEG_INF, 0.0, cnt))
        x = jnp.where(eq, NEG_INF, x)
    return vals, cnts


def _pair_sums(at, bt):
    s8 = SUBLANES
    return jnp.concatenate([at[k:k + 1] + bt[0:s8] for k in range(s8)]
                           + [at[0:1] + bt[s8:], at[s8:] + bt[0:1]], axis=0)


def _kth_largest(x, mult, k):
    tau = jnp.zeros_like(x[0:1])
    cum = jnp.zeros_like(x[0:1])
    for _ in range(k):
        m = jnp.max(x, axis=0, keepdims=True)
        eq = x == m
        tau = jnp.where(cum < k, m, tau)
        cum = cum + jnp.sum(jnp.where(eq, mult, 0.0), axis=0, keepdims=True)
        x = jnp.where(eq, NEG_INF, x)
    return tau


def _route_kernel(q_ref, keys_ref, a_ref, b_ref, rows_ref):
    half = PEER_DQ // 2
    s1 = _dot_nt(keys_ref[0, 0], q_ref[:, 0:half])
    s2 = _dot_nt(keys_ref[0, 1], q_ref[:, half:PEER_DQ])
    v1, n1 = _extract_distinct(s1, PEER_TOPK)
    v2, n2 = _extract_distinct(s2, PEER_TOPK)
    at = jnp.concatenate([(v - v1[0]) * LOG2E for v in v1], axis=0)
    bt = jnp.concatenate([(v - v2[0]) * LOG2E for v in v2], axis=0)
    s8 = SUBLANES
    na = jnp.concatenate(n1, axis=0)
    nb = jnp.concatenate(n2, axis=0)
    mult = jnp.concatenate([na[k:k + 1] * nb[0:s8] for k in range(s8)]
                           + [na[0:1] * nb[s8:], na[s8:] * nb[0:1]], axis=0)
    cand = _pair_sums(at, bt)
    tau = _kth_largest(cand, mult, PEER_TOPK)
    z = jnp.sum(jnp.where(cand >= tau, mult * jnp.exp2(cand), 0.0), axis=0, keepdims=True)
    nlz = -jnp.log2(z)
    a_ref[0] = (s1 - v1[0]) * LOG2E
    b_ref[0] = (s2 - v2[0]) * LOG2E + nlz
    tau_z = _kth_largest(_pair_sums(at, bt + nlz), mult, PEER_TOPK)
    rows_ref[0] = jnp.concatenate([tau_z, jnp.zeros((s8 - 1, tau_z.shape[1]), F32)], axis=0)


def peer_route(q, keys, *, tt=256):
    t = q.shape[0]
    tt = _tile(t, tt)
    tab = jax.ShapeDtypeStruct((PEER_HEADS, PEER_NKEYS, t), F32)
    tab_spec = pl.BlockSpec((1, PEER_NKEYS, tt), lambda i, p: (p, 0, i))
    return pl.pallas_call(
        _route_kernel,
        grid=(t // tt, PEER_HEADS),
        in_specs=[pl.BlockSpec((tt, PEER_DQ), lambda i, p: (i, p)),
                  pl.BlockSpec((1, 2, PEER_NKEYS, PEER_DQ // 2), lambda i, p: (p, 0, 0, 0))],
        out_specs=[tab_spec, tab_spec, pl.BlockSpec((1, SUBLANES, tt), lambda i, p: (p, 0, i))],
        out_shape=[tab, tab, jax.ShapeDtypeStruct((PEER_HEADS, SUBLANES, t), F32)],
        compiler_params=_params("parallel", "parallel"),
        name="peer_route",
    )(q, keys)


def _gelu_tanh(x):
    return x * (0.5 * (1.0 + jnp.tanh(math.sqrt(2.0 / math.pi) * (x + 0.044715 * (x * x * x)))))


def _peer_kernel(x_ref, gf_ref, u_ref, v_ref, a_ref, b_ref, rows_ref, gfin_ref,
                 o_ref, h_ref, acc_ref, gate_ref, coef_ref):
    e = pl.program_id(1)
    te, tt = u_ref.shape[0], x_ref.shape[0]
    half = te // 2
    rows = half // PEER_NKEYS

    @pl.when(e == 0)
    def _():
        h_ref[...] = _rms(x_ref[...], gf_ref[...]).astype(BF16)
        acc_ref[...] = jnp.zeros_like(acc_ref)

    for hf in range(2):
        for il in range(rows):
            rs = slice(il * PEER_NKEYS, (il + 1) * PEER_NKEYS)
            i = e * (2 * rows) + hf * rows + il
            a_rows = [a_ref[p, pl.ds(i, 1), :] for p in range(PEER_HEADS)]
            for lb in range(tt // LANES):
                ls = slice(lb * LANES, (lb + 1) * LANES)
                g = None
                for p in range(PEER_HEADS):
                    pair = a_rows[p][:, ls] + b_ref[p, :, ls]
                    w = jnp.exp2(jnp.where(pair >= rows_ref[p, 0:1, ls], pair, NEG_INF))
                    g = w if g is None else g + w
                gate_ref[hf, rs, ls] = g
        act = _gelu_tanh(_dot_nt(u_ref[hf * half:(hf + 1) * half, :], h_ref[...]))
        coef_ref[hf] = (act * gate_ref[hf]).astype(BF16)
        acc_ref[...] += _dot_tn(coef_ref[hf], v_ref[hf * half:(hf + 1) * half, :])

    @pl.when(e == pl.num_programs(1) - 1)
    def _():
        o_ref[...] = _rms(x_ref[...] + acc_ref[...], gfin_ref[...])


def peer_experts(x1, gain_ffn, u, v, a, b, rows, gain_final, *, tt=512, te=1024):
    t, d = x1.shape
    tt, te = _tile(t, tt), _tile(PEER_EXPERTS, te)
    tab_spec = pl.BlockSpec((PEER_HEADS, PEER_NKEYS, tt), lambda i, e: (0, 0, i))
    return pl.pallas_call(
        _peer_kernel,
        grid=(t // tt, PEER_EXPERTS // te),
        in_specs=[pl.BlockSpec((tt, d), lambda i, e: (i, 0), pipeline_mode=pl.Buffered(1)),
                  pl.BlockSpec((1, d), lambda i, e: (0, 0)),
                  pl.BlockSpec((te, d), lambda i, e: (e, 0)),
                  pl.BlockSpec((te, d), lambda i, e: (e, 0)),
                  tab_spec, tab_spec,
                  pl.BlockSpec((PEER_HEADS, SUBLANES, tt), lambda i, e: (0, 0, i)),
                  pl.BlockSpec((1, d), lambda i, e: (0, 0))],
        out_specs=pl.BlockSpec((tt, d), lambda i, e: (i, 0)),
        out_shape=jax.ShapeDtypeStruct((t, d), F32),
        scratch_shapes=[pltpu.VMEM((tt, d), BF16),
                        pltpu.VMEM((tt, d), F32),
                        pltpu.VMEM((2, te // 2, tt), F32),
                        pltpu.VMEM((2, te // 2, tt), BF16)],
        compiler_params=_params("parallel", "arbitrary"),
        name="peer_experts",
    )(x1, gain_ffn.reshape(1, d), u, v, a, b, rows, gain_final.reshape(1, d))


def _prepare_weights(w_in, gdn_a_log, gdn_dt_bias, w_branch_a, w_branch_b, w_out, peer_w_q,
                     peer_sub_keys, peer_u, peer_v):
    o_z = GDN_QKV + GDN_WIDTH
    o_ab = o_z + 4 * GDN_HEADS
    o_b = o_ab + 2 * DIFF_QK + DIFF_WIDTH
    w = w_in[0]
    pad = jnp.zeros((D_MODEL, LANES - 4 * GDN_HEADS), F32)
    row = lambda p: jnp.concatenate([p[0].reshape(1, -1), jnp.zeros((1, LANES - 2 * GDN_HEADS), F32)], axis=1)
    return dict(
        w_a=jnp.concatenate([w[:, :o_z], w[:, o_b:]], axis=1).astype(BF16),
        w_ab=jnp.concatenate([w[:, o_z:o_ab], pad], axis=1).astype(BF16),
        w_b=w[:, o_ab:o_b].astype(BF16),
        alog=row(gdn_a_log), dtb=row(gdn_dt_bias),
        qscale=jnp.concatenate([jnp.full((1, DIFF_QK), DIFF_HD ** -0.5 * LOG2E, F32),
                                jnp.ones((1, DIFF_QK + DIFF_WIDTH), F32)], axis=1),
        w_branch_a=w_branch_a[0].astype(BF16), w_branch_b=w_branch_b[0].astype(BF16),
        w_out=w_out[0].astype(BF16), w_q=peer_w_q[0].astype(BF16),
        keys=peer_sub_keys[0].astype(BF16), u=peer_u[0].astype(BF16), v=peer_v[0].astype(BF16),
    )


def _trunk(x, pw, norm_mix_gain, conv_w, gdn_norm_gain, diff_lambda, diff_norm_gain,
           norm_ffn_gain, norm_final_gain, slopes):
    b, s, d = x.shape
    t = b * s
    xt = x.reshape(t, d)
    g_mix = norm_mix_gain[0]

    proj_a = norm_matmul(xt, g_mix, pw["w_a"], F32, name="in_proj_a")
    proj_b = norm_matmul(xt, g_mix, pw["w_b"], BF16, epilogue=_scale_epilogue,
                         extra=(pw["qscale"],), group=2 * DIFF_HD, name="in_proj_b")
    gb = norm_matmul(xt, g_mix, pw["w_ab"], F32, epilogue=_gate_epilogue,
                     extra=(pw["alog"], pw["dtb"]), name="in_proj_gates")

    qkv_n = conv_prep(proj_a.reshape(b, s, -1), conv_w[0])
    g = gb[:, :2 * GDN_HEADS].reshape(b, s, 2, GDN_HEADS)
    beta = gb[:, 2 * GDN_HEADS:4 * GDN_HEADS].reshape(b, s, 2, GDN_HEADS)
    gb_col = jnp.transpose(jnp.concatenate([g, beta], axis=-1), (0, 2, 1, 3))
    gb_col = jnp.pad(gb_col, ((0, 0), (0, 0), (0, 0), (0, LANES - 2 * GDN_HEADS)))
    g_row = jnp.transpose(g, (0, 2, 3, 1))
    o_gdn = gdn_scan(qkv_n, gb_col, g_row).reshape(2, t, GDN_WIDTH)

    o_b = diff_attention(proj_b.reshape(-1, b, s, 2 * DIFF_HD), slopes, diff_lambda[0], diff_norm_gain[0])
    o_b = o_b.reshape(t, DIFF_WIDTH)

    merged = merge_branches(o_gdn, proj_a, o_b, pw["w_branch_a"], pw["w_branch_b"], gdn_norm_gain[0])
    x1 = matmul_residual(merged, pw["w_out"], xt, name="out_proj")

    q = norm_matmul(x1, norm_ffn_gain[0], pw["w_q"], BF16, name="peer_query")
    a, bb, rows = peer_route(q, pw["keys"])
    y = peer_experts(x1, norm_ffn_gain[0], pw["u"], pw["v"], a, bb, rows, norm_final_gain)
    return y.reshape(b, s, d)


def kernel(x_prompt, x_sample, norm_mix_gain, w_in, conv_w, gdn_a_log, gdn_dt_bias, gdn_norm_gain,
           diff_lambda, diff_norm_gain, w_branch_a, w_branch_b, w_out, norm_ffn_gain,
           peer_w_q, peer_sub_keys, peer_u, peer_v, norm_final_gain):
    pw = _prepare_weights(w_in, gdn_a_log, gdn_dt_bias, w_branch_a, w_branch_b, w_out, peer_w_q,
                          peer_sub_keys, peer_u, peer_v)
    slopes = 2.0 ** (-8.0 * jnp.arange(1, DIFF_HEADS + 1, dtype=F32) / DIFF_HEADS)
    run = functools.partial(_trunk, pw=pw, norm_mix_gain=norm_mix_gain, conv_w=conv_w,
                            gdn_norm_gain=gdn_norm_gain, diff_lambda=diff_lambda,
                            diff_norm_gain=diff_norm_gain, norm_ffn_gain=norm_ffn_gain,
                            norm_final_gain=norm_final_gain, slopes=slopes)
    return (run(x_prompt), run(x_sample))
```

```python
import functools
import math

import jax
import jax.numpy as jnp
from jax import lax
from jax.experimental import pallas as pl
from jax.experimental.pallas import tpu as pltpu

F32 = jnp.float32
BF16 = jnp.bfloat16

D_MODEL = 2048
GDN_HEADS = 8
GDN_DK = 128
GDN_QK = 1024
GDN_WIDTH = 1024
GDN_QKV = 3072
GDN_CONV = 5
GDN_CHUNK = 128
DIFF_HEADS = 4
DIFF_HD = 128
DIFF_QK = 1024
DIFF_WIDTH = 1024
PEER_HEADS = 8
PEER_NKEYS = 128
PEER_EXPERTS = PEER_NKEYS * PEER_NKEYS
PEER_DQ = 256
PEER_TOPK = 16
NORM_EPS = 1e-6
LAM_INIT = 0.8 - 0.6 * math.exp(-0.3 * 0)

LANES = 128
SUBLANES = 8
VMEM_LIMIT_BYTES = 56 * 1024 * 1024

NEG_INF = float("-inf")
LOG2E = math.log2(math.e)


def _params(*sem):
    return pltpu.CompilerParams(dimension_semantics=sem, vmem_limit_bytes=VMEM_LIMIT_BYTES)


def _tile(n, pref):
    t = min(n, pref)
    while n % t:
        t //= 2
    return t


def _dot(a, b):
    return jnp.dot(a, b, preferred_element_type=F32)


def _dot_nt(a, b):
    return lax.dot_general(a, b, (((1,), (1,)), ((), ())), preferred_element_type=F32)


def _dot_tn(a, b):
    return lax.dot_general(a, b, (((0,), (0,)), ((), ())), preferred_element_type=F32)


def _sigmoid(x):
    return 1.0 / (1.0 + jnp.exp(-x))


def _rms(x, gain):
    return x * lax.rsqrt(jnp.mean(x * x, axis=-1, keepdims=True) + NORM_EPS) * gain


def _norm_matmul_kernel(x_ref, g_ref, w_ref, *rest, epilogue, n_extra):
    extra, o_ref, h_ref = rest[:n_extra], rest[n_extra], rest[n_extra + 1]

    @pl.when(pl.program_id(1) == 0)
    def _():
        h_ref[...] = _rms(x_ref[...], g_ref[...]).astype(BF16)

    acc = _dot(h_ref[...], w_ref[...])
    if epilogue is not None:
        acc = epilogue(acc, *[e[...] for e in extra])
    if len(o_ref.shape) == 2:
        o_ref[...] = acc.astype(o_ref.dtype)
    else:
        gw = o_ref.shape[2]
        for gi in range(o_ref.shape[0]):
            o_ref[gi] = acc[:, gi * gw:(gi + 1) * gw].astype(o_ref.dtype)


def norm_matmul(x, gain, w, out_dtype, *, epilogue=None, extra=(), tm=1024, tn=1024, group=None, name):
    t, d = x.shape
    n = w.shape[1]
    tm, tn = _tile(t, tm), _tile(n, tn)
    kern = functools.partial(_norm_matmul_kernel, epilogue=epilogue, n_extra=len(extra))
    if group is None:
        out_spec = pl.BlockSpec((tm, tn), lambda i, j: (i, j))
        out_shape = jax.ShapeDtypeStruct((t, n), out_dtype)
    else:
        out_spec = pl.BlockSpec((tn // group, tm, group), lambda i, j: (j, i, 0))
        out_shape = jax.ShapeDtypeStruct((n // group, t, group), out_dtype)
    return pl.pallas_call(
        kern,
        grid=(t // tm, n // tn),
        in_specs=[pl.BlockSpec((tm, d), lambda i, j: (i, 0)),
                  pl.BlockSpec((1, d), lambda i, j: (0, 0)),
                  pl.BlockSpec((d, tn), lambda i, j: (0, j))]
                 + [pl.BlockSpec((1, tn), lambda i, j: (0, j)) for _ in extra],
        out_specs=out_spec,
        out_shape=out_shape,
        scratch_shapes=[pltpu.VMEM((tm, d), BF16)],
        compiler_params=_params("parallel", "arbitrary"),
        name=name,
    )(x, gain.reshape(1, d), w, *extra)


def _gate_epilogue(acc, alog, dtb):
    z = acc + dtb
    softplus = jnp.maximum(z, 0.0) + jnp.log(1.0 + jnp.exp(-jnp.abs(z)))
    g = -jnp.exp(alog) * softplus
    lane = lax.broadcasted_iota(jnp.int32, acc.shape, 1)
    return jnp.where(lane < 2 * GDN_HEADS, g, _sigmoid(acc))


def _scale_epilogue(acc, scale):
    return acc * scale


def _matmul_res_kernel(a_ref, w_ref, r_ref, o_ref):
    o_ref[...] = r_ref[...] + _dot(a_ref[...], w_ref[...])


def matmul_residual(a, w, res, *, tm=1024, tn=1024, name):
    t, k = a.shape
    n = w.shape[1]
    tm, tn = _tile(t, tm), _tile(n, tn)
    return pl.pallas_call(
        _matmul_res_kernel,
        grid=(t // tm, n // tn),
        in_specs=[pl.BlockSpec((tm, k), lambda i, j: (i, 0)),
                  pl.BlockSpec((k, tn), lambda i, j: (0, j)),
                  pl.BlockSpec((tm, tn), lambda i, j: (i, j))],
        out_specs=pl.BlockSpec((tm, tn), lambda i, j: (i, j)),
        out_shape=jax.ShapeDtypeStruct((t, n), F32),
        compiler_params=_params("parallel", "arbitrary"),
        name=name,
    )(a, w, res)


def _conv_kernel(cur_ref, prev_ref, next_ref, w_ref, o_ref, pad_ref):
    i, c = pl.program_id(1), pl.program_id(2)
    ts = cur_ref.shape[1]
    halo = SUBLANES
    pad_ref[0:halo, :] = jnp.where(i > 0, prev_ref[0], 0.0)
    pad_ref[halo:halo + ts, :] = cur_ref[0]
    pad_ref[halo + ts:2 * halo + ts, :] = jnp.where(i < pl.num_programs(1) - 1, next_ref[0], 0.0)
    w = w_ref[...]
    first = halo - (GDN_CONV - 1) // 2
    acc = w[0:1, :] * pad_ref[first:first + ts, :]
    for k in range(1, GDN_CONV):
        acc = acc + w[k:k + 1, :] * pad_ref[first + k:first + k + ts, :]
    y = acc * _sigmoid(acc)
    rs = lax.rsqrt(jnp.sum(y * y, axis=-1, keepdims=True) + 1e-6)
    f = jnp.where(c < GDN_HEADS, rs * (GDN_DK ** -0.5), jnp.where(c < 2 * GDN_HEADS, rs, 1.0))
    o_ref[0] = (y * f).astype(o_ref.dtype)


def conv_prep(proj_a, conv_w, *, ts=2048):
    b, s, _ = proj_a.shape
    ts = _tile(s, ts)
    r = ts // SUBLANES
    last = s // SUBLANES - 1
    return pl.pallas_call(
        _conv_kernel,
        grid=(b, s // ts, GDN_QKV // LANES),
        in_specs=[pl.BlockSpec((1, ts, LANES), lambda b_, i, c: (b_, i, c)),
                  pl.BlockSpec((1, SUBLANES, LANES), lambda b_, i, c: (b_, jnp.maximum(i * r - 1, 0), c)),
                  pl.BlockSpec((1, SUBLANES, LANES), lambda b_, i, c: (b_, jnp.minimum((i + 1) * r, last), c)),
                  pl.BlockSpec((GDN_CONV, LANES), lambda b_, i, c: (0, c))],
        out_specs=pl.BlockSpec((1, ts, LANES), lambda b_, i, c: (b_, i, c)),
        out_shape=jax.ShapeDtypeStruct((b, s, GDN_QKV), BF16),
        scratch_shapes=[pltpu.VMEM((ts + 2 * SUBLANES, LANES), F32)],
        compiler_params=_params("parallel", "parallel", "parallel"),
        name="gdn_conv_prep",
    )(proj_a, proj_a, proj_a, conv_w)


def _split3(x):
    hi = x.astype(BF16)
    r = x - hi.astype(F32)
    mid = r.astype(BF16)
    lo = (r - mid.astype(F32)).astype(BF16)
    return hi, mid, lo


def _gdn_kernel(q_ref, k_ref, v_ref, gb_ref, gr_ref, o_ref, state_ref):
    d, i = pl.program_id(1), pl.program_id(2)
    c = GDN_CHUNK

    @pl.when(i == 0)
    def _():
        state_ref[...] = jnp.zeros_like(state_ref)

    row = lax.broadcasted_iota(jnp.int32, (c, c), 0)
    col = lax.broadcasted_iota(jnp.int32, (c, c), 1)
    order = jnp.where(d == 0, row - col, col - row)
    incl = order >= 0
    strict = order > 0
    eye = jnp.where(row == col, 1.0, 0.0)
    ones_incl = jnp.where(incl, 1.0, 0.0).astype(BF16)

    gb = gb_ref[0, 0]
    g_col = gb[:, 0:GDN_HEADS]
    g_row = gr_ref[0, 0]
    gcum_col = sum(_dot(ones_incl, p) for p in _split3(g_col))
    gcum_row = sum(_dot_nt(p, ones_incl) for p in _split3(g_row))
    g_tot = jnp.sum(g_col, axis=0, keepdims=True)
    beta_all = gb[:, GDN_HEADS:2 * GDN_HEADS]

    heads = range(GDN_HEADS)
    hs = [slice(h * GDN_DK, (h + 1) * GDN_DK) for h in heads]
    q = [q_ref[0, :, hs[h]] for h in heads]
    k = [k_ref[0, :, hs[h]] for h in heads]
    beta = [beta_all[:, h:h + 1] for h in heads]
    gcc = [gcum_col[:, h:h + 1] for h in heads]
    gt = [g_tot[:, h:h + 1] for h in heads]
    kf = [k[h].astype(F32) for h in heads]
    kb = [kf[h] * beta[h] for h in heads]
    akk = [_dot_nt(kb[h].astype(BF16), k[h]) for h in heads]
    qk = [_dot_nt(q[h], k[h]) for h in heads]
    decay = [jnp.where(incl, jnp.exp(jnp.where(incl, gcc[h] - gcum_row[h:h + 1, :], 0.0)), 0.0)
             for h in heads]

    p = [-jnp.where(strict, akk[h] * decay[h], 0.0) for h in heads]
    t_inv = [eye + p[h] for h in heads]
    for _ in range(int(math.log2(c)) - 1):
        pb = [p[h].astype(BF16) for h in heads]
        p = [_dot(pb[h], pb[h]) for h in heads]
        t_inv = [t_inv[h] + _dot(t_inv[h].astype(BF16), p[h].astype(BF16)) for h in heads]

    eg = [jnp.exp(gcc[h]) for h in heads]
    rhs = [jnp.concatenate([(v_ref[0, :, hs[h]].astype(F32) * beta[h]).astype(BF16),
                            (kb[h] * eg[h]).astype(BF16)], axis=1) for h in heads]
    uw = [_dot(t_inv[h].astype(BF16), rhs[h]) for h in heads]
    lhs_o = [jnp.concatenate([(q[h].astype(F32) * eg[h]).astype(BF16),
                              jnp.where(incl, qk[h] * decay[h], 0.0).astype(BF16)], axis=1) for h in heads]
    k_tail = [(kf[h] * jnp.exp(gt[h] - gcc[h])).astype(BF16) for h in heads]
    s = [state_ref[h] for h in heads]
    sb = [s[h].astype(BF16) for h in heads]
    vb = [(uw[h][:, :GDN_DK] - _dot(uw[h][:, GDN_DK:].astype(BF16), sb[h])).astype(BF16) for h in heads]
    o = [_dot(lhs_o[h], jnp.concatenate([sb[h], vb[h]], axis=0)) for h in heads]
    s_new = [s[h] * jnp.exp(gt[h]) + _dot_tn(k_tail[h], vb[h]) for h in heads]
    for h in heads:
        state_ref[h] = s_new[h]
        o_ref[0, 0, :, hs[h]] = o[h]


def gdn_scan(qkv_n, gb_col, g_row):
    b, s, _ = qkv_n.shape
    c = GDN_CHUNK
    nc = s // c

    def cidx(d, i):
        return i + d * (nc - 1 - 2 * i)

    def qkv_spec(part):
        return pl.BlockSpec((1, c, GDN_QK), lambda b_, d, i: (b_, cidx(d, i), part))

    col_spec = pl.BlockSpec((1, 1, c, LANES), lambda b_, d, i: (b_, d, cidx(d, i), 0))
    row_spec = pl.BlockSpec((1, 1, GDN_HEADS, c), lambda b_, d, i: (b_, d, 0, cidx(d, i)))
    return pl.pallas_call(
        _gdn_kernel,
        grid=(b, 2, nc),
        in_specs=[qkv_spec(0), qkv_spec(1), qkv_spec(2), col_spec, row_spec],
        out_specs=pl.BlockSpec((1, 1, c, GDN_WIDTH), lambda b_, d, i: (d, b_, cidx(d, i), 0)),
        out_shape=jax.ShapeDtypeStruct((2, b, s, GDN_WIDTH), F32),
        scratch_shapes=[pltpu.VMEM((GDN_HEADS, GDN_DK, GDN_DK), F32)],
        compiler_params=_params("parallel", "parallel", "arbitrary"),
        name="gdn_scan",
    )(qkv_n, qkv_n, qkv_n, gb_col, g_row)


def _attn_kernel(slopes_ref, q_ref, k_ref, v_ref, lam_ref, gain_ref, o_ref, m_ref, l_ref, acc_ref, tile_ref):
    h, qi, sweep, ki = pl.program_id(1), pl.program_id(2), pl.program_id(3), pl.program_id(4)
    tq = tk = q_ref.shape[2]
    sub_blocks = k_ref.shape[2] // tk
    groups = tk // LANES
    c2 = slopes_ref[h] * LOG2E * jnp.ones((1, LANES), F32)

    @pl.when((sweep == 0) & (ki == 0))
    def _():
        m_ref[...] = jnp.full_like(m_ref, NEG_INF)
        rel = (lax.broadcasted_iota(jnp.int32, (tq, tk), 1)
               - lax.broadcasted_iota(jnp.int32, (tq, tk), 0)).astype(F32)
        t0 = c2[:, 0:1] * rel
        tile_ref[0] = t0
        tile_ref[1] = -t0
        tile_ref[2] = -jnp.abs(t0)

    @pl.when((sweep == 1) & (ki == 0))
    def _():
        l_ref[...] = jnp.zeros_like(l_ref)
        acc_ref[...] = jnp.zeros_like(acc_ref)
        for c in range(2):
            m_ref[c] = jnp.broadcast_to(jnp.max(m_ref[c], axis=-1, keepdims=True), (tq, LANES))

    def block_bias(j):
        kj = ki * sub_blocks + j
        case = jnp.where(kj < qi, 0, jnp.where(kj > qi, 1, 2))
        gap = jnp.abs(qi - kj) * tq * jnp.ones((1, LANES), jnp.int32)
        return case, -c2 * gap.astype(F32)

    def scores(c, j, case):
        cs = slice(c * DIFF_HD, (c + 1) * DIFF_HD)
        return _dot_nt(q_ref[0, 0, :, cs], k_ref[0, 0, j * tk:(j + 1) * tk, cs]) + tile_ref[case]

    def lane_groups(x):
        return [x[:, g * LANES:(g + 1) * LANES] for g in range(groups)]

    @pl.when(sweep == 0)
    def _():
        m = [m_ref[c] for c in range(2)]
        for j in range(sub_blocks):
            case, cb = block_bias(j)
            for c in range(2):
                m[c] = jnp.maximum(m[c], functools.reduce(jnp.maximum, lane_groups(scores(c, j, case))) + cb)
        for c in range(2):
            m_ref[c] = m[c]

    @pl.when(sweep == 1)
    def _():
        m = [m_ref[c] for c in range(2)]
        l = [l_ref[c] for c in range(2)]
        pv = [None, None]
        for j in range(sub_blocks):
            case, cb = block_bias(j)
            v = v_ref[0, 0, j * tk:(j + 1) * tk, :]
            for c in range(2):
                shift = jnp.concatenate([cb - m[c]] * groups, axis=1)
                p = jnp.exp2(scores(c, j, case) + shift)
                l[c] = l[c] + functools.reduce(jnp.add, lane_groups(p))
                part = _dot(p.astype(BF16), v)
                pv[c] = part if pv[c] is None else pv[c] + part
        for c in range(2):
            l_ref[c] = l[c]
            acc_ref[c] += pv[c]

    @pl.when((sweep == 1) & (ki == pl.num_programs(4) - 1))
    def _():
        lp = lam_ref[...]
        lam = (jnp.exp(jnp.sum(lp[0:1] * lp[1:2], axis=-1, keepdims=True))
               - jnp.exp(jnp.sum(lp[2:3] * lp[3:4], axis=-1, keepdims=True)) + LAM_INIT)
        l0 = jnp.sum(l_ref[0], axis=-1, keepdims=True)
        l1 = jnp.sum(l_ref[1], axis=-1, keepdims=True)
        o = acc_ref[0] / l0 - lam * (acc_ref[1] / l1)
        o_ref[0] = (_rms(o, gain_ref[...]) * (1.0 - LAM_INIT)).astype(o_ref.dtype)


def diff_attention(proj_b, slopes, diff_lambda, gain, *, tile=512, kv_blocks=4):
    _, b, s, e = proj_b.shape
    tq = tk = _tile(s, tile)
    tkv = _tile(s, kv_blocks * tk)
    return pl.pallas_call(
        _attn_kernel,
        grid=(b, DIFF_HEADS, s // tq, 2, s // tkv),
        in_specs=[pl.BlockSpec(memory_space=pltpu.SMEM),
                  pl.BlockSpec((1, 1, tq, e), lambda b_, h, qi, sw, ki: (h, b_, qi, 0)),
                  pl.BlockSpec((1, 1, tkv, e), lambda b_, h, qi, sw, ki: (DIFF_HEADS + h, b_, ki, 0)),
                  pl.BlockSpec((1, 1, tkv, e), lambda b_, h, qi, sw, ki: (2 * DIFF_HEADS + h, b_, ki * sw, 0)),
                  pl.BlockSpec((4, DIFF_HD), lambda b_, h, qi, sw, ki: (0, 0)),
                  pl.BlockSpec((1, e), lambda b_, h, qi, sw, ki: (0, 0))],
        out_specs=pl.BlockSpec((1, tq, e), lambda b_, h, qi, sw, ki: (b_, qi, h)),
        out_shape=jax.ShapeDtypeStruct((b, s, DIFF_WIDTH), BF16),
        scratch_shapes=[pltpu.VMEM((2, tq, LANES), F32),
                        pltpu.VMEM((2, tq, LANES), F32),
                        pltpu.VMEM((2, tq, e), F32),
                        pltpu.VMEM((3, tq, tk), F32)],
        compiler_params=_params("parallel", "parallel", "parallel", "arbitrary", "arbitrary"),
        name="diff_attention",
    )(slopes, proj_b, proj_b, proj_b, diff_lambda, gain.reshape(1, e))


def _merge_kernel(o_ref, z_ref, ga_ref, gb_ref, ob_ref, wa_ref, wb_ref, gain_ref, out_ref, oa_ref):
    @pl.when(pl.program_id(1) == 0)
    def _():
        gain = gain_ref[...]
        for h in range(GDN_HEADS):
            hs = slice(h * GDN_DK, (h + 1) * GDN_DK)
            z = z_ref[:, hs]
            oa_ref[:, hs] = (_rms(o_ref[0, :, hs] + o_ref[1, :, hs], gain) * (z * _sigmoid(z))).astype(BF16)

    out_ref[...] = (_sigmoid(ga_ref[...]) * _dot(oa_ref[...], wa_ref[...])
                    + _sigmoid(gb_ref[...]) * _dot(ob_ref[...], wb_ref[...])).astype(out_ref.dtype)


def merge_branches(o_gdn, proj_a, o_b, w_a, w_b, gain, *, tm=512, tn=1024):
    t = o_b.shape[0]
    tm = _tile(t, tm)
    z_blk = GDN_QKV // GDN_WIDTH
    ga_blk = (GDN_QKV + GDN_WIDTH) // tn
    gb_blk = (GDN_QKV + GDN_WIDTH + D_MODEL) // tn
    return pl.pallas_call(
        _merge_kernel,
        grid=(t // tm, D_MODEL // tn),
        in_specs=[pl.BlockSpec((2, tm, GDN_WIDTH), lambda i, j: (0, i, 0)),
                  pl.BlockSpec((tm, GDN_WIDTH), lambda i, j: (i, z_blk)),
                  pl.BlockSpec((tm, tn), lambda i, j: (i, ga_blk + j)),
                  pl.BlockSpec((tm, tn), lambda i, j: (i, gb_blk + j)),
                  pl.BlockSpec((tm, DIFF_WIDTH), lambda i, j: (i, 0)),
                  pl.BlockSpec((GDN_WIDTH, tn), lambda i, j: (0, j)),
                  pl.BlockSpec((DIFF_WIDTH, tn), lambda i, j: (0, j)),
                  pl.BlockSpec((1, GDN_DK), lambda i, j: (0, 0))],
        out_specs=pl.BlockSpec((tm, tn), lambda i, j: (i, j)),
        out_shape=jax.ShapeDtypeStruct((t, D_MODEL), BF16),
        scratch_shapes=[pltpu.VMEM((tm, GDN_WIDTH), BF16)],
        compiler_params=_params("parallel", "arbitrary"),
        name="merge_branches",
    )(o_gdn, proj_a, proj_a, proj_a, o_b, w_a, w_b, gain.reshape(1, GDN_DK))


def _extract_distinct(x, n):
    vals, cnts = [], []
    for _ in range(n):
        m = jnp.max(x, axis=0, keepdims=True)
        eq = x == m
        cnt = jnp.sum(jnp.where(eq, 1.0, 0.0), axis=0, keepdims=True)
        vals.append(m)
        cnts.append(jnp.where(m == NEG_INF, 0.0, cnt))
        x = jnp.where(eq, NEG_INF, x)
    return vals, cnts


def _pair_sums(at, bt):
    s8 = SUBLANES
    return jnp.concatenate([at[k:k + 1] + bt[0:s8] for k in range(s8)]
                           + [at[0:1] + bt[s8:], at[s8:] + bt[0:1]], axis=0)


def _kth_largest(x, mult, k):
    tau = jnp.zeros_like(x[0:1])
    cum = jnp.zeros_like(x[0:1])
    for _ in range(k):
        m = jnp.max(x, axis=0, keepdims=True)
        eq = x == m
        tau = jnp.where(cum < k, m, tau)
        cum = cum + jnp.sum(jnp.where(eq, mult, 0.0), axis=0, keepdims=True)
        x = jnp.where(eq, NEG_INF, x)
    return tau


def _route_kernel(q_ref, keys_ref, a_ref, b_ref, rows_ref):
    half = PEER_DQ // 2
    s1 = _dot_nt(keys_ref[0, 0], q_ref[:, 0:half])
    s2 = _dot_nt(keys_ref[0, 1], q_ref[:, half:PEER_DQ])
    v1, n1 = _extract_distinct(s1, PEER_TOPK)
    v2, n2 = _extract_distinct(s2, PEER_TOPK)
    at = jnp.concatenate([(v - v1[0]) * LOG2E for v in v1], axis=0)
    bt = jnp.concatenate([(v - v2[0]) * LOG2E for v in v2], axis=0)
    s8 = SUBLANES
    na = jnp.concatenate(n1, axis=0)
    nb = jnp.concatenate(n2, axis=0)
    mult = jnp.concatenate([na[k:k + 1] * nb[0:s8] for k in range(s8)]
                           + [na[0:1] * nb[s8:], na[s8:] * nb[0:1]], axis=0)
    cand = _pair_sums(at, bt)
    tau = _kth_largest(cand, mult, PEER_TOPK)
    z = jnp.sum(jnp.where(cand >= tau, mult * jnp.exp2(cand), 0.0), axis=0, keepdims=True)
    nlz = -jnp.log2(z)
    a_ref[0] = (s1 - v1[0]) * LOG2E
    b_ref[0] = (s2 - v2[0]) * LOG2E + nlz
    tau_z = _kth_largest(_pair_sums(at, bt + nlz), mult, PEER_TOPK)
    rows_ref[0] = jnp.concatenate([tau_z, jnp.zeros((s8 - 1, tau_z.shape[1]), F32)], axis=0)


def peer_route(q, keys, *, tt=512):
    t = q.shape[0]
    tt = _tile(t, tt)
    tab = jax.ShapeDtypeStruct((PEER_HEADS, PEER_NKEYS, t), F32)
    tab_spec = pl.BlockSpec((1, PEER_NKEYS, tt), lambda i, p: (p, 0, i))
    return pl.pallas_call(
        _route_kernel,
        grid=(t // tt, PEER_HEADS),
        in_specs=[pl.BlockSpec((tt, PEER_DQ), lambda i, p: (i, p)),
                  pl.BlockSpec((1, 2, PEER_NKEYS, PEER_DQ // 2), lambda i, p: (p, 0, 0, 0))],
        out_specs=[tab_spec, tab_spec, pl.BlockSpec((1, SUBLANES, tt), lambda i, p: (p, 0, i))],
        out_shape=[tab, tab, jax.ShapeDtypeStruct((PEER_HEADS, SUBLANES, t), F32)],
        compiler_params=_params("parallel", "parallel"),
        name="peer_route",
    )(q, keys)


def _gelu_tanh(x):
    return x * (0.5 * (1.0 + jnp.tanh(math.sqrt(2.0 / math.pi) * (x + 0.044715 * (x * x * x)))))


def _peer_kernel(x_ref, gf_ref, u_ref, v_ref, a_ref, b_ref, rows_ref, gfin_ref,
                 o_ref, h_ref, acc_ref, gate_ref, coef_ref):
    e = pl.program_id(1)
    te, tt = u_ref.shape[0], x_ref.shape[0]
    half = te // 2
    rows = half // PEER_NKEYS

    @pl.when(e == 0)
    def _():
        h_ref[...] = _rms(x_ref[...], gf_ref[...]).astype(BF16)
        acc_ref[...] = jnp.zeros_like(acc_ref)

    for hf in range(2):
        for il in range(rows):
            rs = slice(il * PEER_NKEYS, (il + 1) * PEER_NKEYS)
            i = e * (2 * rows) + hf * rows + il
            a_rows = [a_ref[p, pl.ds(i, 1), :] for p in range(PEER_HEADS)]
            for lb in range(tt // LANES):
                ls = slice(lb * LANES, (lb + 1) * LANES)
                g = None
                for p in range(PEER_HEADS):
                    pair = a_rows[p][:, ls] + b_ref[p, :, ls]
                    w = jnp.exp2(jnp.where(pair >= rows_ref[p, 0:1, ls], pair, NEG_INF))
                    g = w if g is None else g + w
                gate_ref[hf, rs, ls] = g
        act = _gelu_tanh(_dot_nt(u_ref[hf * half:(hf + 1) * half, :], h_ref[...]))
        coef_ref[hf] = (act * gate_ref[hf]).astype(BF16)
        acc_ref[...] += _dot_tn(coef_ref[hf], v_ref[hf * half:(hf + 1) * half, :])

    @pl.when(e == pl.num_programs(1) - 1)
    def _():
        o_ref[...] = _rms(x_ref[...] + acc_ref[...], gfin_ref[...])


def peer_experts(x1, gain_ffn, u, v, a, b, rows, gain_final, *, tt=512, te=1024):
    t, d = x1.shape
    tt, te = _tile(t, tt), _tile(PEER_EXPERTS, te)
    tab_spec = pl.BlockSpec((PEER_HEADS, PEER_NKEYS, tt), lambda i, e: (0, 0, i))
    return pl.pallas_call(
        _peer_kernel,
        grid=(t // tt, PEER_EXPERTS // te),
        in_specs=[pl.BlockSpec((tt, d), lambda i, e: (i, 0), pipeline_mode=pl.Buffered(1)),
                  pl.BlockSpec((1, d), lambda i, e: (0, 0)),
                  pl.BlockSpec((te, d), lambda i, e: (e, 0)),
                  pl.BlockSpec((te, d), lambda i, e: (e, 0)),
                  tab_spec, tab_spec,
                  pl.BlockSpec((PEER_HEADS, SUBLANES, tt), lambda i, e: (0, 0, i)),
                  pl.BlockSpec((1, d), lambda i, e: (0, 0))],
        out_specs=pl.BlockSpec((tt, d), lambda i, e: (i, 0)),
        out_shape=jax.ShapeDtypeStruct((t, d), F32),
        scratch_shapes=[pltpu.VMEM((tt, d), BF16),
                        pltpu.VMEM((tt, d), F32),
                        pltpu.VMEM((2, te // 2, tt), F32),
                        pltpu.VMEM((2, te // 2, tt), BF16)],
        compiler_params=_params("parallel", "arbitrary"),
        name="peer_experts",
    )(x1, gain_ffn.reshape(1, d), u, v, a, b, rows, gain_final.reshape(1, d))


def _prepare_weights(w_in, gdn_a_log, gdn_dt_bias, w_branch_a, w_branch_b, w_out, peer_w_q,
                     peer_sub_keys, peer_u, peer_v):
    o_z = GDN_QKV + GDN_WIDTH
    o_ab = o_z + 4 * GDN_HEADS
    o_b = o_ab + 2 * DIFF_QK + DIFF_WIDTH
    w = w_in[0]
    pad = jnp.zeros((D_MODEL, LANES - 4 * GDN_HEADS), F32)
    row = lambda p: jnp.concatenate([p[0].reshape(1, -1), jnp.zeros((1, LANES - 2 * GDN_HEADS), F32)], axis=1)
    return dict(
        w_a=jnp.concatenate([w[:, :o_z], w[:, o_b:]], axis=1).astype(BF16),
        w_ab=jnp.concatenate([w[:, o_z:o_ab], pad], axis=1).astype(BF16),
        w_b=w[:, o_ab:o_b].astype(BF16),
        alog=row(gdn_a_log), dtb=row(gdn_dt_bias),
        qscale=jnp.concatenate([jnp.full((1, DIFF_QK), DIFF_HD ** -0.5 * LOG2E, F32),
                                jnp.ones((1, DIFF_QK + DIFF_WIDTH), F32)], axis=1),
        w_branch_a=w_branch_a[0].astype(BF16), w_branch_b=w_branch_b[0].astype(BF16),
        w_out=w_out[0].astype(BF16), w_q=peer_w_q[0].astype(BF16),
        keys=peer_sub_keys[0].astype(BF16), u=peer_u[0].astype(BF16), v=peer_v[0].astype(BF16),
    )


def _trunk(x, pw, norm_mix_gain, conv_w, gdn_norm_gain, diff_lambda, diff_norm_gain,
           norm_ffn_gain, norm_final_gain, slopes):
    b, s, d = x.shape
    t = b * s
    xt = x.reshape(t, d)
    g_mix = norm_mix_gain[0]

    proj_a = norm_matmul(xt, g_mix, pw["w_a"], F32, name="in_proj_a")
    proj_b = norm_matmul(xt, g_mix, pw["w_b"], BF16, epilogue=_scale_epilogue,
                         extra=(pw["qscale"],), group=2 * DIFF_HD, name="in_proj_b")
    gb = norm_matmul(xt, g_mix, pw["w_ab"], F32, epilogue=_gate_epilogue,
                     extra=(pw["alog"], pw["dtb"]), name="in_proj_gates")

    qkv_n = conv_prep(proj_a.reshape(b, s, -1), conv_w[0])
    g = gb[:, :2 * GDN_HEADS].reshape(b, s, 2, GDN_HEADS)
    beta = gb[:, 2 * GDN_HEADS:4 * GDN_HEADS].reshape(b, s, 2, GDN_HEADS)
    gb_col = jnp.transpose(jnp.concatenate([g, beta], axis=-1), (0, 2, 1, 3))
    gb_col = jnp.pad(gb_col, ((0, 0), (0, 0), (0, 0), (0, LANES - 2 * GDN_HEADS)))
    g_row = jnp.transpose(g, (0, 2, 3, 1))
    o_gdn = gdn_scan(qkv_n, gb_col, g_row).reshape(2, t, GDN_WIDTH)

    o_b = diff_attention(proj_b.reshape(-1, b, s, 2 * DIFF_HD), slopes, diff_lambda[0], diff_norm_gain[0])
    o_b = o_b.reshape(t, DIFF_WIDTH)

    merged = merge_branches(o_gdn, proj_a, o_b, pw["w_branch_a"], pw["w_branch_b"], gdn_norm_gain[0])
    x1 = matmul_residual(merged, pw["w_out"], xt, name="out_proj")

    q = norm_matmul(x1, norm_ffn_gain[0], pw["w_q"], BF16, name="peer_query")
    a, bb, rows = peer_route(q, pw["keys"])
    y = peer_experts(x1, norm_ffn_gain[0], pw["u"], pw["v"], a, bb, rows, norm_final_gain)
    return y.reshape(b, s, d)


def kernel(x_prompt, x_sample, norm_mix_gain, w_in, conv_w, gdn_a_log, gdn_dt_bias, gdn_norm_gain,
           diff_lambda, diff_norm_gain, w_branch_a, w_branch_b, w_out, norm_ffn_gain,
           peer_w_q, peer_sub_keys, peer_u, peer_v, norm_final_gain):
    pw = _prepare_weights(w_in, gdn_a_log, gdn_dt_bias, w_branch_a, w_branch_b, w_out, peer_w_q,
                          peer_sub_keys, peer_u, peer_v)
    slopes = 2.0 ** (-8.0 * jnp.arange(1, DIFF_HEADS + 1, dtype=F32) / DIFF_HEADS)
    run = functools.partial(_trunk, pw=pw, norm_mix_gain=norm_mix_gain, conv_w=conv_w,
                            gdn_norm_gain=gdn_norm_gain, diff_lambda=diff_lambda,
                            diff_norm_gain=diff_norm_gain, norm_ffn_gain=norm_ffn_gain,
                            norm_final_gain=norm_final_gain, slopes=slopes)
    return (run(x_prompt), run(x_sample))
```

```python
import functools
import math

import jax
import jax.numpy as jnp
from jax import lax
from jax.experimental import pallas as pl
from jax.experimental.pallas import tpu as pltpu

F32 = jnp.float32
BF16 = jnp.bfloat16

D_MODEL = 2048
GDN_HEADS = 8
GDN_DK = 128
GDN_QK = 1024
GDN_WIDTH = 1024
GDN_QKV = 3072
GDN_CONV = 5
GDN_CHUNK = 128
DIFF_HEADS = 4
DIFF_HD = 128
DIFF_QK = 1024
DIFF_WIDTH = 1024
PEER_HEADS = 8
PEER_NKEYS = 128
PEER_EXPERTS = PEER_NKEYS * PEER_NKEYS
PEER_DQ = 256
PEER_TOPK = 16
NORM_EPS = 1e-6
LAM_INIT = 0.8 - 0.6 * math.exp(-0.3 * 0)

LANES = 128
SUBLANES = 8
VMEM_LIMIT_BYTES = 56 * 1024 * 1024

NEG_INF = float("-inf")
LOG2E = math.log2(math.e)


def _params(*sem):
    return pltpu.CompilerParams(dimension_semantics=sem, vmem_limit_bytes=VMEM_LIMIT_BYTES)


def _tile(n, pref):
    t = min(n, pref)
    while n % t:
        t //= 2
    return t


def _dot(a, b):
    return jnp.dot(a, b, preferred_element_type=F32)


def _dot_nt(a, b):
    return lax.dot_general(a, b, (((1,), (1,)), ((), ())), preferred_element_type=F32)


def _dot_tn(a, b):
    return lax.dot_general(a, b, (((0,), (0,)), ((), ())), preferred_element_type=F32)


def _sigmoid(x):
    return 1.0 / (1.0 + jnp.exp(-x))


def _rms(x, gain):
    return x * lax.rsqrt(jnp.mean(x * x, axis=-1, keepdims=True) + NORM_EPS) * gain


def _norm_matmul_kernel(x_ref, g_ref, w_ref, *rest, epilogue, n_extra):
    extra, o_ref, h_ref = rest[:n_extra], rest[n_extra], rest[n_extra + 1]

    @pl.when(pl.program_id(1) == 0)
    def _():
        h_ref[...] = _rms(x_ref[...], g_ref[...]).astype(BF16)

    acc = _dot(h_ref[...], w_ref[...])
    if epilogue is not None:
        acc = epilogue(acc, *[e[...] for e in extra])
    if len(o_ref.shape) == 2:
        o_ref[...] = acc.astype(o_ref.dtype)
    else:
        gw = o_ref.shape[2]
        for gi in range(o_ref.shape[0]):
            o_ref[gi] = acc[:, gi * gw:(gi + 1) * gw].astype(o_ref.dtype)


def norm_matmul(x, gain, w, out_dtype, *, epilogue=None, extra=(), tm=1024, tn=1024, group=None, name):
    t, d = x.shape
    n = w.shape[1]
    tm, tn = _tile(t, tm), _tile(n, tn)
    kern = functools.partial(_norm_matmul_kernel, epilogue=epilogue, n_extra=len(extra))
    if group is None:
        out_spec = pl.BlockSpec((tm, tn), lambda i, j: (i, j))
        out_shape = jax.ShapeDtypeStruct((t, n), out_dtype)
    else:
        out_spec = pl.BlockSpec((tn // group, tm, group), lambda i, j: (j, i, 0))
        out_shape = jax.ShapeDtypeStruct((n // group, t, group), out_dtype)
    return pl.pallas_call(
        kern,
        grid=(t // tm, n // tn),
        in_specs=[pl.BlockSpec((tm, d), lambda i, j: (i, 0)),
                  pl.BlockSpec((1, d), lambda i, j: (0, 0)),
                  pl.BlockSpec((d, tn), lambda i, j: (0, j))]
                 + [pl.BlockSpec((1, tn), lambda i, j: (0, j)) for _ in extra],
        out_specs=out_spec,
        out_shape=out_shape,
        scratch_shapes=[pltpu.VMEM((tm, d), BF16)],
        compiler_params=_params("parallel", "arbitrary"),
        name=name,
    )(x, gain.reshape(1, d), w, *extra)


def _gate_epilogue(acc, alog, dtb):
    z = acc + dtb
    softplus = jnp.maximum(z, 0.0) + jnp.log(1.0 + jnp.exp(-jnp.abs(z)))
    g = -jnp.exp(alog) * softplus
    lane = lax.broadcasted_iota(jnp.int32, acc.shape, 1)
    return jnp.where(lane < 2 * GDN_HEADS, g, _sigmoid(acc))


def _scale_epilogue(acc, scale):
    return acc * scale


def _matmul_res_kernel(a_ref, w_ref, r_ref, o_ref):
    o_ref[...] = r_ref[...] + _dot(a_ref[...], w_ref[...])


def matmul_residual(a, w, res, *, tm=1024, tn=1024, name):
    t, k = a.shape
    n = w.shape[1]
    tm, tn = _tile(t, tm), _tile(n, tn)
    return pl.pallas_call(
        _matmul_res_kernel,
        grid=(t // tm, n // tn),
        in_specs=[pl.BlockSpec((tm, k), lambda i, j: (i, 0)),
                  pl.BlockSpec((k, tn), lambda i, j: (0, j)),
                  pl.BlockSpec((tm, tn), lambda i, j: (i, j))],
        out_specs=pl.BlockSpec((tm, tn), lambda i, j: (i, j)),
        out_shape=jax.ShapeDtypeStruct((t, n), F32),
        compiler_params=_params("parallel", "arbitrary"),
        name=name,
    )(a, w, res)


def _conv_kernel(cur_ref, prev_ref, next_ref, w_ref, o_ref, pad_ref):
    i, c = pl.program_id(1), pl.program_id(2)
    ts = cur_ref.shape[1]
    halo = SUBLANES
    pad_ref[0:halo, :] = jnp.where(i > 0, prev_ref[0], 0.0)
    pad_ref[halo:halo + ts, :] = cur_ref[0]
    pad_ref[halo + ts:2 * halo + ts, :] = jnp.where(i < pl.num_programs(1) - 1, next_ref[0], 0.0)
    w = w_ref[...]
    first = halo - (GDN_CONV - 1) // 2
    acc = w[0:1, :] * pad_ref[first:first + ts, :]
    for k in range(1, GDN_CONV):
        acc = acc + w[k:k + 1, :] * pad_ref[first + k:first + k + ts, :]
    y = acc * _sigmoid(acc)
    rs = lax.rsqrt(jnp.sum(y * y, axis=-1, keepdims=True) + 1e-6)
    f = jnp.where(c < GDN_HEADS, rs * (GDN_DK ** -0.5), jnp.where(c < 2 * GDN_HEADS, rs, 1.0))
    o_ref[0] = (y * f).astype(o_ref.dtype)


def conv_prep(proj_a, conv_w, *, ts=2048):
    b, s, _ = proj_a.shape
    ts = _tile(s, ts)
    r = ts // SUBLANES
    last = s // SUBLANES - 1
    return pl.pallas_call(
        _conv_kernel,
        grid=(b, s // ts, GDN_QKV // LANES),
        in_specs=[pl.BlockSpec((1, ts, LANES), lambda b_, i, c: (b_, i, c)),
                  pl.BlockSpec((1, SUBLANES, LANES), lambda b_, i, c: (b_, jnp.maximum(i * r - 1, 0), c)),
                  pl.BlockSpec((1, SUBLANES, LANES), lambda b_, i, c: (b_, jnp.minimum((i + 1) * r, last), c)),
                  pl.BlockSpec((GDN_CONV, LANES), lambda b_, i, c: (0, c))],
        out_specs=pl.BlockSpec((1, ts, LANES), lambda b_, i, c: (b_, i, c)),
        out_shape=jax.ShapeDtypeStruct((b, s, GDN_QKV), BF16),
        scratch_shapes=[pltpu.VMEM((ts + 2 * SUBLANES, LANES), F32)],
        compiler_params=_params("parallel", "parallel", "parallel"),
        name="gdn_conv_prep",
    )(proj_a, proj_a, proj_a, conv_w)


def _split3(x):
    hi = x.astype(BF16)
    r = x - hi.astype(F32)
    mid = r.astype(BF16)
    lo = (r - mid.astype(F32)).astype(BF16)
    return hi, mid, lo


def _gdn_kernel(q_ref, k_ref, v_ref, gb_ref, gr_ref, o_ref, state_ref):
    d, i = pl.program_id(1), pl.program_id(2)
    c = GDN_CHUNK

    @pl.when(i == 0)
    def _():
        state_ref[...] = jnp.zeros_like(state_ref)

    row = lax.broadcasted_iota(jnp.int32, (c, c), 0)
    col = lax.broadcasted_iota(jnp.int32, (c, c), 1)
    order = jnp.where(d == 0, row - col, col - row)
    incl = order >= 0
    strict = order > 0
    eye = jnp.where(row == col, 1.0, 0.0)
    ones_incl = jnp.where(incl, 1.0, 0.0).astype(BF16)

    gb = gb_ref[0, 0]
    g_col = gb[:, 0:GDN_HEADS]
    g_row = gr_ref[0, 0]
    gcum_col = sum(_dot(ones_incl, p) for p in _split3(g_col))
    gcum_row = sum(_dot_nt(p, ones_incl) for p in _split3(g_row))
    g_tot = jnp.sum(g_col, axis=0, keepdims=True)
    beta_all = gb[:, GDN_HEADS:2 * GDN_HEADS]

    heads = range(GDN_HEADS)
    hs = [slice(h * GDN_DK, (h + 1) * GDN_DK) for h in heads]
    q = [q_ref[0, :, hs[h]] for h in heads]
    k = [k_ref[0, :, hs[h]] for h in heads]
    beta = [beta_all[:, h:h + 1] for h in heads]
    gcc = [gcum_col[:, h:h + 1] for h in heads]
    gt = [g_tot[:, h:h + 1] for h in heads]
    kf = [k[h].astype(F32) for h in heads]
    kb = [kf[h] * beta[h] for h in heads]
    akk = [_dot_nt(kb[h].astype(BF16), k[h]) for h in heads]
    qk = [_dot_nt(q[h], k[h]) for h in heads]
    decay = [jnp.where(incl, jnp.exp(jnp.where(incl, gcc[h] - gcum_row[h:h + 1, :], 0.0)), 0.0)
             for h in heads]

    p = [-jnp.where(strict, akk[h] * decay[h], 0.0) for h in heads]
    t_inv = [eye + p[h] for h in heads]
    for _ in range(int(math.log2(c)) - 1):
        pb = [p[h].astype(BF16) for h in heads]
        p = [_dot(pb[h], pb[h]) for h in heads]
        t_inv = [t_inv[h] + _dot(t_inv[h].astype(BF16), p[h].astype(BF16)) for h in heads]

    eg = [jnp.exp(gcc[h]) for h in heads]
    rhs = [jnp.concatenate([(v_ref[0, :, hs[h]].astype(F32) * beta[h]).astype(BF16),
                            (kb[h] * eg[h]).astype(BF16)], axis=1) for h in heads]
    uw = [_dot(t_inv[h].astype(BF16), rhs[h]) for h in heads]
    lhs_o = [jnp.concatenate([(q[h].astype(F32) * eg[h]).astype(BF16),
                              jnp.where(incl, qk[h] * decay[h], 0.0).astype(BF16)], axis=1) for h in heads]
    k_tail = [(kf[h] * jnp.exp(gt[h] - gcc[h])).astype(BF16) for h in heads]
    s = [state_ref[h] for h in heads]
    sb = [s[h].astype(BF16) for h in heads]
    vb = [(uw[h][:, :GDN_DK] - _dot(uw[h][:, GDN_DK:].astype(BF16), sb[h])).astype(BF16) for h in heads]
    o = [_dot(lhs_o[h], jnp.concatenate([sb[h], vb[h]], axis=0)) for h in heads]
    s_new = [s[h] * jnp.exp(gt[h]) + _dot_tn(k_tail[h], vb[h]) for h in heads]
    for h in heads:
        state_ref[h] = s_new[h]
        o_ref[0, 0, :, hs[h]] = o[h]


def gdn_scan(qkv_n, gb_col, g_row):
    b, s, _ = qkv_n.shape
    c = GDN_CHUNK
    nc = s // c

    def cidx(d, i):
        return i + d * (nc - 1 - 2 * i)

    def qkv_spec(part):
        return pl.BlockSpec((1, c, GDN_QK), lambda b_, d, i: (b_, cidx(d, i), part))

    col_spec = pl.BlockSpec((1, 1, c, LANES), lambda b_, d, i: (b_, d, cidx(d, i), 0))
    row_spec = pl.BlockSpec((1, 1, GDN_HEADS, c), lambda b_, d, i: (b_, d, 0, cidx(d, i)))
    return pl.pallas_call(
        _gdn_kernel,
        grid=(b, 2, nc),
        in_specs=[qkv_spec(0), qkv_spec(1), qkv_spec(2), col_spec, row_spec],
        out_specs=pl.BlockSpec((1, 1, c, GDN_WIDTH), lambda b_, d, i: (d, b_, cidx(d, i), 0)),
        out_shape=jax.ShapeDtypeStruct((2, b, s, GDN_WIDTH), F32),
        scratch_shapes=[pltpu.VMEM((GDN_HEADS, GDN_DK, GDN_DK), F32)],
        compiler_params=_params("parallel", "parallel", "arbitrary"),
        name="gdn_scan",
    )(qkv_n, qkv_n, qkv_n, gb_col, g_row)


def _attn_kernel(slopes_ref, q_ref, k_ref, v_ref, lam_ref, gain_ref, o_ref, m_ref, l_ref, acc_ref, tile_ref):
    h, qi, sweep, ki = pl.program_id(1), pl.program_id(2), pl.program_id(3), pl.program_id(4)
    tq = tk = q_ref.shape[2]
    sub_blocks = k_ref.shape[2] // tk
    groups = tk // LANES
    c2 = slopes_ref[h] * LOG2E * jnp.ones((1, LANES), F32)

    @pl.when((sweep == 0) & (ki == 0))
    def _():
        m_ref[...] = jnp.full_like(m_ref, NEG_INF)
        rel = (lax.broadcasted_iota(jnp.int32, (tq, tk), 1)
               - lax.broadcasted_iota(jnp.int32, (tq, tk), 0)).astype(F32)
        t0 = c2[:, 0:1] * rel
        tile_ref[0] = t0
        tile_ref[1] = -t0
        tile_ref[2] = -jnp.abs(t0)

    @pl.when((sweep == 1) & (ki == 0))
    def _():
        l_ref[...] = jnp.zeros_like(l_ref)
        acc_ref[...] = jnp.zeros_like(acc_ref)
        for c in range(2):
            m_ref[c] = jnp.broadcast_to(jnp.max(m_ref[c], axis=-1, keepdims=True), (tq, LANES))

    def block_bias(j):
        kj = ki * sub_blocks + j
        case = jnp.where(kj < qi, 0, jnp.where(kj > qi, 1, 2))
        gap = jnp.abs(qi - kj) * tq * jnp.ones((1, LANES), jnp.int32)
        return case, -c2 * gap.astype(F32)

    def scores(c, j, case):
        cs = slice(c * DIFF_HD, (c + 1) * DIFF_HD)
        return _dot_nt(q_ref[0, 0, :, cs], k_ref[0, 0, j * tk:(j + 1) * tk, cs]) + tile_ref[case]

    def lane_groups(x):
        return [x[:, g * LANES:(g + 1) * LANES] for g in range(groups)]

    @pl.when(sweep == 0)
    def _():
        m = [m_ref[c] for c in range(2)]
        for j in range(sub_blocks):
            case, cb = block_bias(j)
            for c in range(2):
                m[c] = jnp.maximum(m[c], functools.reduce(jnp.maximum, lane_groups(scores(c, j, case))) + cb)
        for c in range(2):
            m_ref[c] = m[c]

    @pl.when(sweep == 1)
    def _():
        m = [m_ref[c] for c in range(2)]
        l = [l_ref[c] for c in range(2)]
        pv = [None, None]
        for j in range(sub_blocks):
            case, cb = block_bias(j)
            v = v_ref[0, 0, j * tk:(j + 1) * tk, :]
            for c in range(2):
                shift = jnp.concatenate([cb - m[c]] * groups, axis=1)
                p = jnp.exp2(scores(c, j, case) + shift)
                l[c] = l[c] + functools.reduce(jnp.add, lane_groups(p))
                part = _dot(p.astype(BF16), v)
                pv[c] = part if pv[c] is None else pv[c] + part
        for c in range(2):
            l_ref[c] = l[c]
            acc_ref[c] += pv[c]

    @pl.when((sweep == 1) & (ki == pl.num_programs(4) - 1))
    def _():
        lp = lam_ref[...]
        lam = (jnp.exp(jnp.sum(lp[0:1] * lp[1:2], axis=-1, keepdims=True))
               - jnp.exp(jnp.sum(lp[2:3] * lp[3:4], axis=-1, keepdims=True)) + LAM_INIT)
        l0 = jnp.sum(l_ref[0], axis=-1, keepdims=True)
        l1 = jnp.sum(l_ref[1], axis=-1, keepdims=True)
        o = acc_ref[0] / l0 - lam * (acc_ref[1] / l1)
        o_ref[0] = (_rms(o, gain_ref[...]) * (1.0 - LAM_INIT)).astype(o_ref.dtype)


def diff_attention(proj_b, slopes, diff_lambda, gain, *, tile=512, kv_blocks=4):
    _, b, s, e = proj_b.shape
    tq = tk = _tile(s, tile)
    tkv = _tile(s, kv_blocks * tk)
    return pl.pallas_call(
        _attn_kernel,
        grid=(b, DIFF_HEADS, s // tq, 2, s // tkv),
        in_specs=[pl.BlockSpec(memory_space=pltpu.SMEM),
                  pl.BlockSpec((1, 1, tq, e), lambda b_, h, qi, sw, ki: (h, b_, qi, 0)),
                  pl.BlockSpec((1, 1, tkv, e), lambda b_, h, qi, sw, ki: (DIFF_HEADS + h, b_, ki, 0)),
                  pl.BlockSpec((1, 1, tkv, e), lambda b_, h, qi, sw, ki: (2 * DIFF_HEADS + h, b_, ki * sw, 0)),
                  pl.BlockSpec((4, DIFF_HD), lambda b_, h, qi, sw, ki: (0, 0)),
                  pl.BlockSpec((1, e), lambda b_, h, qi, sw, ki: (0, 0))],
        out_specs=pl.BlockSpec((1, tq, e), lambda b_, h, qi, sw, ki: (b_, qi, h)),
        out_shape=jax.ShapeDtypeStruct((b, s, DIFF_WIDTH), BF16),
        scratch_shapes=[pltpu.VMEM((2, tq, LANES), F32),
                        pltpu.VMEM((2, tq, LANES), F32),
                        pltpu.VMEM((2, tq, e), F32),
                        pltpu.VMEM((3, tq, tk), F32)],
        compiler_params=_params("parallel", "parallel", "parallel", "arbitrary", "arbitrary"),
        name="diff_attention",
    )(slopes, proj_b, proj_b, proj_b, diff_lambda, gain.reshape(1, e))


def _merge_kernel(o_ref, z_ref, ga_ref, gb_ref, ob_ref, wa_ref, wb_ref, gain_ref, out_ref, oa_ref):
    @pl.when(pl.program_id(1) == 0)
    def _():
        gain = gain_ref[...]
        for h in range(GDN_HEADS):
            hs = slice(h * GDN_DK, (h + 1) * GDN_DK)
            z = z_ref[:, hs]
            oa_ref[:, hs] = (_rms(o_ref[0, :, hs] + o_ref[1, :, hs], gain) * (z * _sigmoid(z))).astype(BF16)

    out_ref[...] = (_sigmoid(ga_ref[...]) * _dot(oa_ref[...], wa_ref[...])
                    + _sigmoid(gb_ref[...]) * _dot(ob_ref[...], wb_ref[...])).astype(out_ref.dtype)


def merge_branches(o_gdn, proj_a, o_b, w_a, w_b, gain, *, tm=512, tn=1024):
    t = o_b.shape[0]
    tm = _tile(t, tm)
    z_blk = GDN_QKV // GDN_WIDTH
    ga_blk = (GDN_QKV + GDN_WIDTH) // tn
    gb_blk = (GDN_QKV + GDN_WIDTH + D_MODEL) // tn
    return pl.pallas_call(
        _merge_kernel,
        grid=(t // tm, D_MODEL // tn),
        in_specs=[pl.BlockSpec((2, tm, GDN_WIDTH), lambda i, j: (0, i, 0)),
                  pl.BlockSpec((tm, GDN_WIDTH), lambda i, j: (i, z_blk)),
                  pl.BlockSpec((tm, tn), lambda i, j: (i, ga_blk + j)),
                  pl.BlockSpec((tm, tn), lambda i, j: (i, gb_blk + j)),
                  pl.BlockSpec((tm, DIFF_WIDTH), lambda i, j: (i, 0)),
                  pl.BlockSpec((GDN_WIDTH, tn), lambda i, j: (0, j)),
                  pl.BlockSpec((DIFF_WIDTH, tn), lambda i, j: (0, j)),
                  pl.BlockSpec((1, GDN_DK), lambda i, j: (0, 0))],
        out_specs=pl.BlockSpec((tm, tn), lambda i, j: (i, j)),
        out_shape=jax.ShapeDtypeStruct((t, D_MODEL), BF16),
        scratch_shapes=[pltpu.VMEM((tm, GDN_WIDTH), BF16)],
        compiler_params=_params("parallel", "arbitrary"),
        name="merge_branches",
    )(o_gdn, proj_a, proj_a, proj_a, o_b, w_a, w_b, gain.reshape(1, GDN_DK))


def _extract_distinct(x, n):
    vals, cnts = [], []
    for _ in range(n):
        m = jnp.max(x, axis=0, keepdims=True)
        eq = x == m
        cnt = jnp.sum(jnp.where(eq, 1.0, 0.0), axis=0, keepdims=True)
        vals.append(m)
        cnts.append(jnp.where(m == NEG_INF, 0.0, cnt))
        x = jnp.where(eq, NEG_INF, x)
    return vals, cnts


def _pair_sums(at, bt):
    s8 = SUBLANES
    return jnp.concatenate([at[k:k + 1] + bt[0:s8] for k in range(s8)]
                           + [at[0:1] + bt[s8:], at[s8:] + bt[0:1]], axis=0)


def _kth_largest(x, mult, k):
    tau = jnp.zeros_like(x[0:1])
    cum = jnp.zeros_like(x[0:1])
    for _ in range(k):
        m = jnp.max(x, axis=0, keepdims=True)
        eq = x == m
        tau = jnp.where(cum < k, m, tau)
        cum = cum + jnp.sum(jnp.where(eq, mult, 0.0), axis=0, keepdims=True)
        x = jnp.where(eq, NEG_INF, x)
    return tau


def _route_kernel(q_ref, keys_ref, a_ref, b_ref, rows_ref):
    half = PEER_DQ // 2
    s1 = _dot_nt(keys_ref[0, 0], q_ref[:, 0:half])
    s2 = _dot_nt(keys_ref[0, 1], q_ref[:, half:PEER_DQ])
    v1, n1 = _extract_distinct(s1, PEER_TOPK)
    v2, n2 = _extract_distinct(s2, PEER_TOPK)
    at = jnp.concatenate([(v - v1[0]) * LOG2E for v in v1], axis=0)
    bt = jnp.concatenate([(v - v2[0]) * LOG2E for v in v2], axis=0)
    s8 = SUBLANES
    na = jnp.concatenate(n1, axis=0)
    nb = jnp.concatenate(n2, axis=0)
    mult = jnp.concatenate([na[k:k + 1] * nb[0:s8] for k in range(s8)]
                           + [na[0:1] * nb[s8:], na[s8:] * nb[0:1]], axis=0)
    cand = _pair_sums(at, bt)
    tau = _kth_largest(cand, mult, PEER_TOPK)
    z = jnp.sum(jnp.where(cand >= tau, mult * jnp.exp2(cand), 0.0), axis=0, keepdims=True)
    nlz = -jnp.log2(z)
    a_ref[0] = (s1 - v1[0]) * LOG2E
    b_ref[0] = (s2 - v2[0]) * LOG2E + nlz
    tau_z = _kth_largest(_pair_sums(at, bt + nlz), mult, PEER_TOPK)
    rows_ref[0] = jnp.concatenate([tau_z, jnp.zeros((s8 - 1, tau_z.shape[1]), F32)], axis=0)


def peer_route(q, keys, *, tt=512):
    t = q.shape[0]
    tt = _tile(t, tt)
    tab = jax.ShapeDtypeStruct((PEER_HEADS, PEER_NKEYS, t), F32)
    tab_spec = pl.BlockSpec((1, PEER_NKEYS, tt), lambda i, p: (p, 0, i))
    return pl.pallas_call(
        _route_kernel,
        grid=(t // tt, PEER_HEADS),
        in_specs=[pl.BlockSpec((tt, PEER_DQ), lambda i, p: (i, p)),
                  pl.BlockSpec((1, 2, PEER_NKEYS, PEER_DQ // 2), lambda i, p: (p, 0, 0, 0))],
        out_specs=[tab_spec, tab_spec, pl.BlockSpec((1, SUBLANES, tt), lambda i, p: (p, 0, i))],
        out_shape=[tab, tab, jax.ShapeDtypeStruct((PEER_HEADS, SUBLANES, t), F32)],
        compiler_params=_params("parallel", "parallel"),
        name="peer_route",
    )(q, keys)


def _gelu_tanh(x):
    return x * (0.5 * (1.0 + jnp.tanh(math.sqrt(2.0 / math.pi) * (x + 0.044715 * (x * x * x)))))


def _peer_kernel(x_ref, gf_ref, u_ref, v_ref, a_ref, b_ref, rows_ref, gfin_ref,
                 o_ref, h_ref, gate_ref, coef_ref):
    e = pl.program_id(1)
    te, tt = u_ref.shape[0], x_ref.shape[0]
    rows = te // PEER_NKEYS

    @pl.when(e == 0)
    def _():
        h_ref[...] = _rms(x_ref[...], gf_ref[...]).astype(BF16)
        o_ref[...] = jnp.zeros_like(o_ref)

    for il in range(rows):
        rs = slice(il * PEER_NKEYS, (il + 1) * PEER_NKEYS)
        a_rows = [a_ref[p, pl.ds(e * rows + il, 1), :] for p in range(PEER_HEADS)]
        for lb in range(tt // LANES):
            ls = slice(lb * LANES, (lb + 1) * LANES)
            g = None
            for p in range(PEER_HEADS):
                pair = a_rows[p][:, ls] + b_ref[p, :, ls]
                w = jnp.exp2(jnp.where(pair >= rows_ref[p, 0:1, ls], pair, NEG_INF))
                g = w if g is None else g + w
            gate_ref[rs, ls] = g
    act = _gelu_tanh(_dot_nt(u_ref[...], h_ref[...]))
    coef_ref[...] = (act * gate_ref[...]).astype(BF16)
    o_ref[...] += _dot_tn(coef_ref[...], v_ref[...])

    @pl.when(e == pl.num_programs(1) - 1)
    def _():
        o_ref[...] = _rms(x_ref[...] + o_ref[...], gfin_ref[...])


def peer_experts(x1, gain_ffn, u, v, a, b, rows, gain_final, *, tt=1024, te=512):
    t, d = x1.shape
    tt, te = _tile(t, tt), _tile(PEER_EXPERTS, te)
    once = pl.Buffered(1)
    tab_spec = pl.BlockSpec((PEER_HEADS, PEER_NKEYS, tt), lambda i, e: (0, 0, i))
    return pl.pallas_call(
        _peer_kernel,
        grid=(t // tt, PEER_EXPERTS // te),
        in_specs=[pl.BlockSpec((tt, d), lambda i, e: (i, 0), pipeline_mode=once),
                  pl.BlockSpec((1, d), lambda i, e: (0, 0)),
                  pl.BlockSpec((te, d), lambda i, e: (e, 0)),
                  pl.BlockSpec((te, d), lambda i, e: (e, 0)),
                  tab_spec, tab_spec,
                  pl.BlockSpec((PEER_HEADS, SUBLANES, tt), lambda i, e: (0, 0, i)),
                  pl.BlockSpec((1, d), lambda i, e: (0, 0))],
        out_specs=pl.BlockSpec((tt, d), lambda i, e: (i, 0), pipeline_mode=once),
        out_shape=jax.ShapeDtypeStruct((t, d), F32),
        scratch_shapes=[pltpu.VMEM((tt, d), BF16),
                        pltpu.VMEM((te, tt), F32),
                        pltpu.VMEM((te, tt), BF16)],
        compiler_params=_params("parallel", "arbitrary"),
        name="peer_experts",
    )(x1, gain_ffn.reshape(1, d), u, v, a, b, rows, gain_final.reshape(1, d))


def _prepare_weights(w_in, gdn_a_log, gdn_dt_bias, w_branch_a, w_branch_b, w_out, peer_w_q,
                     peer_sub_keys, peer_u, peer_v):
    o_z = GDN_QKV + GDN_WIDTH
    o_ab = o_z + 4 * GDN_HEADS
    o_b = o_ab + 2 * DIFF_QK + DIFF_WIDTH
    w = w_in[0]
    pad = jnp.zeros((D_MODEL, LANES - 4 * GDN_HEADS), F32)
    row = lambda p: jnp.concatenate([p[0].reshape(1, -1), jnp.zeros((1, LANES - 2 * GDN_HEADS), F32)], axis=1)
    return dict(
        w_a=jnp.concatenate([w[:, :o_z], w[:, o_b:]], axis=1).astype(BF16),
        w_ab=jnp.concatenate([w[:, o_z:o_ab], pad], axis=1).astype(BF16),
        w_b=w[:, o_ab:o_b].astype(BF16),
        alog=row(gdn_a_log), dtb=row(gdn_dt_bias),
        qscale=jnp.concatenate([jnp.full((1, DIFF_QK), DIFF_HD ** -0.5 * LOG2E, F32),
                                jnp.ones((1, DIFF_QK + DIFF_WIDTH), F32)], axis=1),
        w_branch_a=w_branch_a[0].astype(BF16), w_branch_b=w_branch_b[0].astype(BF16),
        w_out=w_out[0].astype(BF16), w_q=peer_w_q[0].astype(BF16),
        keys=peer_sub_keys[0].astype(BF16), u=peer_u[0].astype(BF16), v=peer_v[0].astype(BF16),
    )


def _trunk(x, pw, norm_mix_gain, conv_w, gdn_norm_gain, diff_lambda, diff_norm_gain,
           norm_ffn_gain, norm_final_gain, slopes):
    b, s, d = x.shape
    t = b * s
    xt = x.reshape(t, d)
    g_mix = norm_mix_gain[0]

    proj_a = norm_matmul(xt, g_mix, pw["w_a"], F32, name="in_proj_a")
    proj_b = norm_matmul(xt, g_mix, pw["w_b"], BF16, epilogue=_scale_epilogue,
                         extra=(pw["qscale"],), group=2 * DIFF_HD, name="in_proj_b")
    gb = norm_matmul(xt, g_mix, pw["w_ab"], F32, epilogue=_gate_epilogue,
                     extra=(pw["alog"], pw["dtb"]), name="in_proj_gates")

    qkv_n = conv_prep(proj_a.reshape(b, s, -1), conv_w[0])
    g = gb[:, :2 * GDN_HEADS].reshape(b, s, 2, GDN_HEADS)
    beta = gb[:, 2 * GDN_HEADS:4 * GDN_HEADS].reshape(b, s, 2, GDN_HEADS)
    gb_col = jnp.transpose(jnp.concatenate([g, beta], axis=-1), (0, 2, 1, 3))
    gb_col = jnp.pad(gb_col, ((0, 0), (0, 0), (0, 0), (0, LANES - 2 * GDN_HEADS)))
    g_row = jnp.transpose(g, (0, 2, 3, 1))
    o_gdn = gdn_scan(qkv_n, gb_col, g_row).reshape(2, t, GDN_WIDTH)

    o_b = diff_attention(proj_b.reshape(-1, b, s, 2 * DIFF_HD), slopes, diff_lambda[0], diff_norm_gain[0])
    o_b = o_b.reshape(t, DIFF_WIDTH)

    merged = merge_branches(o_gdn, proj_a, o_b, pw["w_branch_a"], pw["w_branch_b"], gdn_norm_gain[0])
    x1 = matmul_residual(merged, pw["w_out"], xt, name="out_proj")

    q = norm_matmul(x1, norm_ffn_gain[0], pw["w_q"], BF16, name="peer_query")
    a, bb, rows = peer_route(q, pw["keys"])
    y = peer_experts(x1, norm_ffn_gain[0], pw["u"], pw["v"], a, bb, rows, norm_final_gain)
    return y.reshape(b, s, d)


def kernel(x_prompt, x_sample, norm_mix_gain, w_in, conv_w, gdn_a_log, gdn_dt_bias, gdn_norm_gain,
           diff_lambda, diff_norm_gain, w_branch_a, w_branch_b, w_out, norm_ffn_gain,
           peer_w_q, peer_sub_keys, peer_u, peer_v, norm_final_gain):
    pw = _prepare_weights(w_in, gdn_a_log, gdn_dt_bias, w_branch_a, w_branch_b, w_out, peer_w_q,
                          peer_sub_keys, peer_u, peer_v)
    slopes = 2.0 ** (-8.0 * jnp.arange(1, DIFF_HEADS + 1, dtype=F32) / DIFF_HEADS)
    run = functools.partial(_trunk, pw=pw, norm_mix_gain=norm_mix_gain, conv_w=conv_w,
                            gdn_norm_gain=gdn_norm_gain, diff_lambda=diff_lambda,
                            diff_norm_gain=diff_norm_gain, norm_ffn_gain=norm_ffn_gain,
                            norm_final_gain=norm_final_gain, slopes=slopes)
    return (run(x_prompt), run(x_sample))
```

```python
import functools
import math

import jax
import jax.numpy as jnp
from jax import lax
from jax.experimental import pallas as pl
from jax.experimental.pallas import tpu as pltpu

F32 = jnp.float32
BF16 = jnp.bfloat16

D_MODEL = 2048
GDN_HEADS = 8
GDN_DK = 128
GDN_QK = 1024
GDN_WIDTH = 1024
GDN_QKV = 3072
GDN_CONV = 5
GDN_CHUNK = 128
DIFF_HEADS = 4
DIFF_HD = 128
DIFF_QK = 1024
DIFF_WIDTH = 1024
PEER_HEADS = 8
PEER_NKEYS = 128
PEER_EXPERTS = PEER_NKEYS * PEER_NKEYS
PEER_DQ = 256
PEER_TOPK = 16
NORM_EPS = 1e-6
LAM_INIT = 0.8 - 0.6 * math.exp(-0.3 * 0)

LANES = 128
SUBLANES = 8
VMEM_LIMIT_BYTES = 56 * 1024 * 1024
PEER_VMEM_LIMIT_BYTES = 60 * 1024 * 1024

NEG_INF = float("-inf")
LOG2E = math.log2(math.e)


def _params(*sem, vmem_limit=VMEM_LIMIT_BYTES):
    return pltpu.CompilerParams(dimension_semantics=sem, vmem_limit_bytes=vmem_limit)


def _tile(n, pref):
    t = min(n, pref)
    while n % t:
        t //= 2
    return t


def _dot(a, b):
    return jnp.dot(a, b, preferred_element_type=F32)


def _dot_nt(a, b):
    return lax.dot_general(a, b, (((1,), (1,)), ((), ())), preferred_element_type=F32)


def _dot_tn(a, b):
    return lax.dot_general(a, b, (((0,), (0,)), ((), ())), preferred_element_type=F32)


def _sigmoid(x):
    return 1.0 / (1.0 + jnp.exp(-x))


def _rms(x, gain):
    return x * lax.rsqrt(jnp.mean(x * x, axis=-1, keepdims=True) + NORM_EPS) * gain


def _norm_matmul_kernel(x_ref, g_ref, w_ref, *rest, epilogue, n_extra):
    extra, o_ref, h_ref = rest[:n_extra], rest[n_extra], rest[n_extra + 1]

    @pl.when(pl.program_id(1) == 0)
    def _():
        h_ref[...] = _rms(x_ref[...], g_ref[...]).astype(BF16)

    acc = _dot(h_ref[...], w_ref[...])
    if epilogue is not None:
        acc = epilogue(acc, *[e[...] for e in extra])
    if len(o_ref.shape) == 2:
        o_ref[...] = acc.astype(o_ref.dtype)
    else:
        gw = o_ref.shape[2]
        for gi in range(o_ref.shape[0]):
            o_ref[gi] = acc[:, gi * gw:(gi + 1) * gw].astype(o_ref.dtype)


def norm_matmul(x, gain, w, out_dtype, *, epilogue=None, extra=(), tm=1024, tn=1024, group=None, name):
    t, d = x.shape
    n = w.shape[1]
    tm, tn = _tile(t, tm), _tile(n, tn)
    kern = functools.partial(_norm_matmul_kernel, epilogue=epilogue, n_extra=len(extra))
    if group is None:
        out_spec = pl.BlockSpec((tm, tn), lambda i, j: (i, j))
        out_shape = jax.ShapeDtypeStruct((t, n), out_dtype)
    else:
        out_spec = pl.BlockSpec((tn // group, tm, group), lambda i, j: (j, i, 0))
        out_shape = jax.ShapeDtypeStruct((n // group, t, group), out_dtype)
    return pl.pallas_call(
        kern,
        grid=(t // tm, n // tn),
        in_specs=[pl.BlockSpec((tm, d), lambda i, j: (i, 0)),
                  pl.BlockSpec((1, d), lambda i, j: (0, 0)),
                  pl.BlockSpec((d, tn), lambda i, j: (0, j))]
                 + [pl.BlockSpec((1, tn), lambda i, j: (0, j)) for _ in extra],
        out_specs=out_spec,
        out_shape=out_shape,
        scratch_shapes=[pltpu.VMEM((tm, d), BF16)],
        compiler_params=_params("parallel", "arbitrary"),
        name=name,
    )(x, gain.reshape(1, d), w, *extra)


def _gate_epilogue(acc, alog, dtb):
    z = acc + dtb
    softplus = jnp.maximum(z, 0.0) + jnp.log(1.0 + jnp.exp(-jnp.abs(z)))
    g = -jnp.exp(alog) * softplus
    lane = lax.broadcasted_iota(jnp.int32, acc.shape, 1)
    return jnp.where(lane < 2 * GDN_HEADS, g, _sigmoid(acc))


def _scale_epilogue(acc, scale):
    return acc * scale


def _matmul_res_kernel(a_ref, w_ref, r_ref, o_ref):
    o_ref[...] = r_ref[...] + _dot(a_ref[...], w_ref[...])


def matmul_residual(a, w, res, *, tm=1024, tn=1024, name):
    t, k = a.shape
    n = w.shape[1]
    tm, tn = _tile(t, tm), _tile(n, tn)
    return pl.pallas_call(
        _matmul_res_kernel,
        grid=(t // tm, n // tn),
        in_specs=[pl.BlockSpec((tm, k), lambda i, j: (i, 0)),
                  pl.BlockSpec((k, tn), lambda i, j: (0, j)),
                  pl.BlockSpec((tm, tn), lambda i, j: (i, j))],
        out_specs=pl.BlockSpec((tm, tn), lambda i, j: (i, j)),
        out_shape=jax.ShapeDtypeStruct((t, n), F32),
        compiler_params=_params("parallel", "arbitrary"),
        name=name,
    )(a, w, res)


def _conv_kernel(cur_ref, prev_ref, next_ref, w_ref, o_ref, pad_ref):
    i, c = pl.program_id(1), pl.program_id(2)
    ts = cur_ref.shape[1]
    halo = SUBLANES
    pad_ref[0:halo, :] = jnp.where(i > 0, prev_ref[0], 0.0)
    pad_ref[halo:halo + ts, :] = cur_ref[0]
    pad_ref[halo + ts:2 * halo + ts, :] = jnp.where(i < pl.num_programs(1) - 1, next_ref[0], 0.0)
    w = w_ref[...]
    first = halo - (GDN_CONV - 1) // 2
    acc = w[0:1, :] * pad_ref[first:first + ts, :]
    for k in range(1, GDN_CONV):
        acc = acc + w[k:k + 1, :] * pad_ref[first + k:first + k + ts, :]
    y = acc * _sigmoid(acc)
    rs = lax.rsqrt(jnp.sum(y * y, axis=-1, keepdims=True) + 1e-6)
    f = jnp.where(c < GDN_HEADS, rs * (GDN_DK ** -0.5), jnp.where(c < 2 * GDN_HEADS, rs, 1.0))
    o_ref[0] = (y * f).astype(o_ref.dtype)


def conv_prep(proj_a, conv_w, *, ts=2048):
    b, s, _ = proj_a.shape
    ts = _tile(s, ts)
    r = ts // SUBLANES
    last = s // SUBLANES - 1
    return pl.pallas_call(
        _conv_kernel,
        grid=(b, s // ts, GDN_QKV // LANES),
        in_specs=[pl.BlockSpec((1, ts, LANES), lambda b_, i, c: (b_, i, c)),
                  pl.BlockSpec((1, SUBLANES, LANES), lambda b_, i, c: (b_, jnp.maximum(i * r - 1, 0), c)),
                  pl.BlockSpec((1, SUBLANES, LANES), lambda b_, i, c: (b_, jnp.minimum((i + 1) * r, last), c)),
                  pl.BlockSpec((GDN_CONV, LANES), lambda b_, i, c: (0, c))],
        out_specs=pl.BlockSpec((1, ts, LANES), lambda b_, i, c: (b_, i, c)),
        out_shape=jax.ShapeDtypeStruct((b, s, GDN_QKV), BF16),
        scratch_shapes=[pltpu.VMEM((ts + 2 * SUBLANES, LANES), F32)],
        compiler_params=_params("parallel", "parallel", "parallel"),
        name="gdn_conv_prep",
    )(proj_a, proj_a, proj_a, conv_w)


def _split3(x):
    hi = x.astype(BF16)
    r = x - hi.astype(F32)
    mid = r.astype(BF16)
    lo = (r - mid.astype(F32)).astype(BF16)
    return hi, mid, lo


def _gdn_kernel(q_ref, k_ref, v_ref, gb_ref, gr_ref, o_ref, state_ref):
    d, i = pl.program_id(1), pl.program_id(2)
    c = GDN_CHUNK

    @pl.when(i == 0)
    def _():
        state_ref[...] = jnp.zeros_like(state_ref)

    row = lax.broadcasted_iota(jnp.int32, (c, c), 0)
    col = lax.broadcasted_iota(jnp.int32, (c, c), 1)
    order = jnp.where(d == 0, row - col, col - row)
    incl = order >= 0
    strict = order > 0
    eye = jnp.where(row == col, 1.0, 0.0)
    ones_incl = jnp.where(incl, 1.0, 0.0).astype(BF16)

    gb = gb_ref[0, 0]
    g_col = gb[:, 0:GDN_HEADS]
    g_row = gr_ref[0, 0]
    gcum_col = sum(_dot(ones_incl, p) for p in _split3(g_col))
    gcum_row = sum(_dot_nt(p, ones_incl) for p in _split3(g_row))
    g_tot = jnp.sum(g_col, axis=0, keepdims=True)
    beta_all = gb[:, GDN_HEADS:2 * GDN_HEADS]

    heads = range(GDN_HEADS)
    hs = [slice(h * GDN_DK, (h + 1) * GDN_DK) for h in heads]
    q = [q_ref[0, :, hs[h]] for h in heads]
    k = [k_ref[0, :, hs[h]] for h in heads]
    beta = [beta_all[:, h:h + 1] for h in heads]
    gcc = [gcum_col[:, h:h + 1] for h in heads]
    gt = [g_tot[:, h:h + 1] for h in heads]
    kf = [k[h].astype(F32) for h in heads]
    kb = [kf[h] * beta[h] for h in heads]
    akk = [_dot_nt(kb[h].astype(BF16), k[h]) for h in heads]
    qk = [_dot_nt(q[h], k[h]) for h in heads]
    decay = [jnp.where(incl, jnp.exp(jnp.where(incl, gcc[h] - gcum_row[h:h + 1, :], 0.0)), 0.0)
             for h in heads]

    p = [-jnp.where(strict, akk[h] * decay[h], 0.0) for h in heads]
    t_inv = [eye + p[h] for h in heads]
    for _ in range(int(math.log2(c)) - 1):
        pb = [p[h].astype(BF16) for h in heads]
        p = [_dot(pb[h], pb[h]) for h in heads]
        t_inv = [t_inv[h] + _dot(t_inv[h].astype(BF16), p[h].astype(BF16)) for h in heads]

    eg = [jnp.exp(gcc[h]) for h in heads]
    rhs = [jnp.concatenate([(v_ref[0, :, hs[h]].astype(F32) * beta[h]).astype(BF16),
                            (kb[h] * eg[h]).astype(BF16)], axis=1) for h in heads]
    uw = [_dot(t_inv[h].astype(BF16), rhs[h]) for h in heads]
    lhs_o = [jnp.concatenate([(q[h].astype(F32) * eg[h]).astype(BF16),
                              jnp.where(incl, qk[h] * decay[h], 0.0).astype(BF16)], axis=1) for h in heads]
    k_tail = [(kf[h] * jnp.exp(gt[h] - gcc[h])).astype(BF16) for h in heads]
    s = [state_ref[h] for h in heads]
    sb = [s[h].astype(BF16) for h in heads]
    vb = [(uw[h][:, :GDN_DK] - _dot(uw[h][:, GDN_DK:].astype(BF16), sb[h])).astype(BF16) for h in heads]
    o = [_dot(lhs_o[h], jnp.concatenate([sb[h], vb[h]], axis=0)) for h in heads]
    s_new = [s[h] * jnp.exp(gt[h]) + _dot_tn(k_tail[h], vb[h]) for h in heads]
    for h in heads:
        state_ref[h] = s_new[h]
        o_ref[0, 0, :, hs[h]] = o[h]


def gdn_scan(qkv_n, gb_col, g_row):
    b, s, _ = qkv_n.shape
    c = GDN_CHUNK
    nc = s // c

    def cidx(d, i):
        return i + d * (nc - 1 - 2 * i)

    def qkv_spec(part):
        return pl.BlockSpec((1, c, GDN_QK), lambda b_, d, i: (b_, cidx(d, i), part))

    col_spec = pl.BlockSpec((1, 1, c, LANES), lambda b_, d, i: (b_, d, cidx(d, i), 0))
    row_spec = pl.BlockSpec((1, 1, GDN_HEADS, c), lambda b_, d, i: (b_, d, 0, cidx(d, i)))
    return pl.pallas_call(
        _gdn_kernel,
        grid=(b, 2, nc),
        in_specs=[qkv_spec(0), qkv_spec(1), qkv_spec(2), col_spec, row_spec],
        out_specs=pl.BlockSpec((1, 1, c, GDN_WIDTH), lambda b_, d, i: (d, b_, cidx(d, i), 0)),
        out_shape=jax.ShapeDtypeStruct((2, b, s, GDN_WIDTH), F32),
        scratch_shapes=[pltpu.VMEM((GDN_HEADS, GDN_DK, GDN_DK), F32)],
        compiler_params=_params("parallel", "parallel", "arbitrary"),
        name="gdn_scan",
    )(qkv_n, qkv_n, qkv_n, gb_col, g_row)


def _attn_kernel(slopes_ref, q_ref, k_ref, v_ref, lam_ref, gain_ref, o_ref, m_ref, l_ref, acc_ref, tile_ref):
    h, qi, sweep, ki = pl.program_id(1), pl.program_id(2), pl.program_id(3), pl.program_id(4)
    tq = tk = q_ref.shape[2]
    sub_blocks = k_ref.shape[2] // tk
    groups = tk // LANES
    c2 = slopes_ref[h] * LOG2E * jnp.ones((1, LANES), F32)

    @pl.when((sweep == 0) & (ki == 0))
    def _():
        m_ref[...] = jnp.full_like(m_ref, NEG_INF)
        rel = (lax.broadcasted_iota(jnp.int32, (tq, tk), 1)
               - lax.broadcasted_iota(jnp.int32, (tq, tk), 0)).astype(F32)
        t0 = c2[:, 0:1] * rel
        tile_ref[0] = t0
        tile_ref[1] = -t0
        tile_ref[2] = -jnp.abs(t0)

    @pl.when((sweep == 1) & (ki == 0))
    def _():
        l_ref[...] = jnp.zeros_like(l_ref)
        acc_ref[...] = jnp.zeros_like(acc_ref)
        for c in range(2):
            m_ref[c] = jnp.broadcast_to(jnp.max(m_ref[c], axis=-1, keepdims=True), (tq, LANES))

    def block_bias(j):
        kj = ki * sub_blocks + j
        case = jnp.where(kj < qi, 0, jnp.where(kj > qi, 1, 2))
        gap = jnp.abs(qi - kj) * tq * jnp.ones((1, LANES), jnp.int32)
        return case, -c2 * gap.astype(F32)

    def scores(c, j, case):
        cs = slice(c * DIFF_HD, (c + 1) * DIFF_HD)
        return _dot_nt(q_ref[0, 0, :, cs], k_ref[0, 0, j * tk:(j + 1) * tk, cs]) + tile_ref[case]

    def lane_groups(x):
        return [x[:, g * LANES:(g + 1) * LANES] for g in range(groups)]

    @pl.when(sweep == 0)
    def _():
        m = [m_ref[c] for c in range(2)]
        for j in range(sub_blocks):
            case, cb = block_bias(j)
            for c in range(2):
                m[c] = jnp.maximum(m[c], functools.reduce(jnp.maximum, lane_groups(scores(c, j, case))) + cb)
        for c in range(2):
            m_ref[c] = m[c]

    @pl.when(sweep == 1)
    def _():
        m = [m_ref[c] for c in range(2)]
        l = [l_ref[c] for c in range(2)]
        pv = [None, None]
        for j in range(sub_blocks):
            case, cb = block_bias(j)
            v = v_ref[0, 0, j * tk:(j + 1) * tk, :]
            for c in range(2):
                shift = jnp.concatenate([cb - m[c]] * groups, axis=1)
                p = jnp.exp2(scores(c, j, case) + shift)
                l[c] = l[c] + functools.reduce(jnp.add, lane_groups(p))
                part = _dot(p.astype(BF16), v)
                pv[c] = part if pv[c] is None else pv[c] + part
        for c in range(2):
            l_ref[c] = l[c]
            acc_ref[c] += pv[c]

    @pl.when((sweep == 1) & (ki == pl.num_programs(4) - 1))
    def _():
        lp = lam_ref[...]
        lam = (jnp.exp(jnp.sum(lp[0:1] * lp[1:2], axis=-1, keepdims=True))
               - jnp.exp(jnp.sum(lp[2:3] * lp[3:4], axis=-1, keepdims=True)) + LAM_INIT)
        l0 = jnp.sum(l_ref[0], axis=-1, keepdims=True)
        l1 = jnp.sum(l_ref[1], axis=-1, keepdims=True)
        o = acc_ref[0] / l0 - lam * (acc_ref[1] / l1)
        o_ref[0] = (_rms(o, gain_ref[...]) * (1.0 - LAM_INIT)).astype(o_ref.dtype)


def diff_attention(proj_b, slopes, diff_lambda, gain, *, tile=512, kv_blocks=8):
    _, b, s, e = proj_b.shape
    tq = tk = _tile(s, tile)
    tkv = _tile(s, kv_blocks * tk)
    return pl.pallas_call(
        _attn_kernel,
        grid=(b, DIFF_HEADS, s // tq, 2, s // tkv),
        in_specs=[pl.BlockSpec(memory_space=pltpu.SMEM),
                  pl.BlockSpec((1, 1, tq, e), lambda b_, h, qi, sw, ki: (h, b_, qi, 0)),
                  pl.BlockSpec((1, 1, tkv, e), lambda b_, h, qi, sw, ki: (DIFF_HEADS + h, b_, ki, 0)),
                  pl.BlockSpec((1, 1, tkv, e), lambda b_, h, qi, sw, ki: (2 * DIFF_HEADS + h, b_, ki * sw, 0)),
                  pl.BlockSpec((4, DIFF_HD), lambda b_, h, qi, sw, ki: (0, 0)),
                  pl.BlockSpec((1, e), lambda b_, h, qi, sw, ki: (0, 0))],
        out_specs=pl.BlockSpec((1, tq, e), lambda b_, h, qi, sw, ki: (b_, qi, h)),
        out_shape=jax.ShapeDtypeStruct((b, s, DIFF_WIDTH), BF16),
        scratch_shapes=[pltpu.VMEM((2, tq, LANES), F32),
                        pltpu.VMEM((2, tq, LANES), F32),
                        pltpu.VMEM((2, tq, e), F32),
                        pltpu.VMEM((3, tq, tk), F32)],
        compiler_params=_params("parallel", "parallel", "parallel", "arbitrary", "arbitrary"),
        name="diff_attention",
    )(slopes, proj_b, proj_b, proj_b, diff_lambda, gain.reshape(1, e))


def _merge_kernel(o_ref, z_ref, ga_ref, gb_ref, ob_ref, wa_ref, wb_ref, gain_ref, out_ref, oa_ref):
    @pl.when(pl.program_id(1) == 0)
    def _():
        gain = gain_ref[...]
        for h in range(GDN_HEADS):
            hs = slice(h * GDN_DK, (h + 1) * GDN_DK)
            z = z_ref[:, hs]
            oa_ref[:, hs] = (_rms(o_ref[0, :, hs] + o_ref[1, :, hs], gain) * (z * _sigmoid(z))).astype(BF16)

    out_ref[...] = (_sigmoid(ga_ref[...]) * _dot(oa_ref[...], wa_ref[...])
                    + _sigmoid(gb_ref[...]) * _dot(ob_ref[...], wb_ref[...])).astype(out_ref.dtype)


def merge_branches(o_gdn, proj_a, o_b, w_a, w_b, gain, *, tm=256, tn=2048):
    t = o_b.shape[0]
    tm = _tile(t, tm)
    z_blk = GDN_QKV // GDN_WIDTH
    ga_blk = (GDN_QKV + GDN_WIDTH) // tn
    gb_blk = (GDN_QKV + GDN_WIDTH + D_MODEL) // tn
    return pl.pallas_call(
        _merge_kernel,
        grid=(t // tm, D_MODEL // tn),
        in_specs=[pl.BlockSpec((2, tm, GDN_WIDTH), lambda i, j: (0, i, 0)),
                  pl.BlockSpec((tm, GDN_WIDTH), lambda i, j: (i, z_blk)),
                  pl.BlockSpec((tm, tn), lambda i, j: (i, ga_blk + j)),
                  pl.BlockSpec((tm, tn), lambda i, j: (i, gb_blk + j)),
                  pl.BlockSpec((tm, DIFF_WIDTH), lambda i, j: (i, 0)),
                  pl.BlockSpec((GDN_WIDTH, tn), lambda i, j: (0, j)),
                  pl.BlockSpec((DIFF_WIDTH, tn), lambda i, j: (0, j)),
                  pl.BlockSpec((1, GDN_DK), lambda i, j: (0, 0))],
        out_specs=pl.BlockSpec((tm, tn), lambda i, j: (i, j)),
        out_shape=jax.ShapeDtypeStruct((t, D_MODEL), BF16),
        scratch_shapes=[pltpu.VMEM((tm, GDN_WIDTH), BF16)],
        compiler_params=_params("parallel", "arbitrary"),
        name="merge_branches",
    )(o_gdn, proj_a, proj_a, proj_a, o_b, w_a, w_b, gain.reshape(1, GDN_DK))


def _extract_distinct(x, n):
    vals, cnts = [], []
    for _ in range(n):
        m = jnp.max(x, axis=0, keepdims=True)
        eq = x == m
        cnt = jnp.sum(jnp.where(eq, 1.0, 0.0), axis=0, keepdims=True)
        vals.append(m)
        cnts.append(jnp.where(m == NEG_INF, 0.0, cnt))
        x = jnp.where(eq, NEG_INF, x)
    return vals, cnts


def _pair_sums(at, bt):
    s8 = SUBLANES
    return jnp.concatenate([at[k:k + 1] + bt[0:s8] for k in range(s8)]
                           + [at[0:1] + bt[s8:], at[s8:] + bt[0:1]], axis=0)


def _kth_largest(x, mult, k):
    tau = jnp.zeros_like(x[0:1])
    cum = jnp.zeros_like(x[0:1])
    for _ in range(k):
        m = jnp.max(x, axis=0, keepdims=True)
        eq = x == m
        tau = jnp.where(cum < k, m, tau)
        cum = cum + jnp.sum(jnp.where(eq, mult, 0.0), axis=0, keepdims=True)
        x = jnp.where(eq, NEG_INF, x)
    return tau


def _route_kernel(q_ref, keys_ref, a_ref, b_ref, rows_ref):
    half = PEER_DQ // 2
    s1 = _dot_nt(keys_ref[0, 0], q_ref[:, 0:half])
    s2 = _dot_nt(keys_ref[0, 1], q_ref[:, half:PEER_DQ])
    v1, n1 = _extract_distinct(s1, PEER_TOPK)
    v2, n2 = _extract_distinct(s2, PEER_TOPK)
    at = jnp.concatenate([(v - v1[0]) * LOG2E for v in v1], axis=0)
    bt = jnp.concatenate([(v - v2[0]) * LOG2E for v in v2], axis=0)
    s8 = SUBLANES
    na = jnp.concatenate(n1, axis=0)
    nb = jnp.concatenate(n2, axis=0)
    mult = jnp.concatenate([na[k:k + 1] * nb[0:s8] for k in range(s8)]
                           + [na[0:1] * nb[s8:], na[s8:] * nb[0:1]], axis=0)
    cand = _pair_sums(at, bt)
    tau = _kth_largest(cand, mult, PEER_TOPK)
    z = jnp.sum(jnp.where(cand >= tau, mult * jnp.exp2(cand), 0.0), axis=0, keepdims=True)
    nlz = -jnp.log2(z)
    a_ref[0] = (s1 - v1[0]) * LOG2E
    b_ref[0] = (s2 - v2[0]) * LOG2E + nlz
    tau_z = jnp.min(jnp.where(cand >= tau, _pair_sums(at, bt + nlz), jnp.inf), axis=0, keepdims=True)
    rows_ref[0] = jnp.concatenate([tau_z, jnp.zeros((s8 - 1, tau_z.shape[1]), F32)], axis=0)


def peer_route(q, keys, *, tt=512):
    t = q.shape[0]
    tt = _tile(t, tt)
    tab = jax.ShapeDtypeStruct((PEER_HEADS, PEER_NKEYS, t), F32)
    tab_spec = pl.BlockSpec((1, PEER_NKEYS, tt), lambda i, p: (p, 0, i))
    return pl.pallas_call(
        _route_kernel,
        grid=(t // tt, PEER_HEADS),
        in_specs=[pl.BlockSpec((tt, PEER_DQ), lambda i, p: (i, p)),
                  pl.BlockSpec((1, 2, PEER_NKEYS, PEER_DQ // 2), lambda i, p: (p, 0, 0, 0))],
        out_specs=[tab_spec, tab_spec, pl.BlockSpec((1, SUBLANES, tt), lambda i, p: (p, 0, i))],
        out_shape=[tab, tab, jax.ShapeDtypeStruct((PEER_HEADS, SUBLANES, t), F32)],
        compiler_params=_params("parallel", "parallel"),
        name="peer_route",
    )(q, keys)


def _gelu_tanh(x):
    return x * (0.5 * (1.0 + jnp.tanh(math.sqrt(2.0 / math.pi) * (x + 0.044715 * (x * x * x)))))


def _peer_kernel(x_ref, gf_ref, u_ref, v_ref, a_ref, b_ref, rows_ref, gfin_ref,
                 o_ref, h_ref, gate_ref, coef_ref):
    e = pl.program_id(1)
    te, tt = u_ref.shape[0], x_ref.shape[0]
    rows = te // PEER_NKEYS

    @pl.when(e == 0)
    def _():
        h_ref[...] = _rms(x_ref[...], gf_ref[...]).astype(BF16)
        o_ref[...] = jnp.zeros_like(o_ref)

    for il in range(rows):
        rs = slice(il * PEER_NKEYS, (il + 1) * PEER_NKEYS)
        a_rows = [a_ref[p, pl.ds(e * rows + il, 1), :] for p in range(PEER_HEADS)]
        for lb in range(tt // LANES):
            ls = slice(lb * LANES, (lb + 1) * LANES)
            g = None
            for p in range(PEER_HEADS):
                pair = a_rows[p][:, ls] + b_ref[p, :, ls]
                w = jnp.exp2(jnp.where(pair >= rows_ref[p, 0:1, ls], pair, NEG_INF))
                g = w if g is None else g + w
            gate_ref[rs, ls] = g
    act =_gelu_tanh(_dot_nt(u_ref[...], h_ref[...]))
    coef_ref[...] = (act * gate_ref[...]).astype(BF16)
    o_ref[...] += _dot_tn(coef_ref[...], v_ref[...])

    @pl.when(e == pl.num_programs(1) - 1)
    def _():
        o_ref[...] = _rms(x_ref[...] + o_ref[...], gfin_ref[...])


def peer_experts(x1, gain_ffn, u, v, a, b, rows, gain_final, *, tt=1024, te=512):
    t, d = x1.shape
    tt, te = _tile(t, tt), _tile(PEER_EXPERTS, te)
    once = pl.Buffered(1)
    tab_spec = pl.BlockSpec((PEER_HEADS, PEER_NKEYS, tt), lambda i, e: (0, 0, i))
    return pl.pallas_call(
        _peer_kernel,
        grid=(t // tt, PEER_EXPERTS // te),
        in_specs=[pl.BlockSpec((tt, d), lambda i, e: (i, 0), pipeline_mode=once),
                  pl.BlockSpec((1, d), lambda i, e: (0, 0)),
                  pl.BlockSpec((te, d), lambda i, e: (e, 0)),
                  pl.BlockSpec((te, d), lambda i, e: (e, 0)),
                  tab_spec, tab_spec,
                  pl.BlockSpec((PEER_HEADS, SUBLANES, tt), lambda i, e: (0, 0, i)),
                  pl.BlockSpec((1, d), lambda i, e: (0, 0))],
        out_specs=pl.BlockSpec((tt, d), lambda i, e: (i, 0), pipeline_mode=once),
        out_shape=jax.ShapeDtypeStruct((t, d), F32),
        scratch_shapes=[pltpu.VMEM((tt, d), BF16),
                        pltpu.VMEM((te, tt), F32),
                        pltpu.VMEM((te, tt), BF16)],
        compiler_params=_params("parallel", "arbitrary", vmem_limit=PEER_VMEM_LIMIT_BYTES),
        name="peer_experts",
    )(x1, gain_ffn.reshape(1, d), u, v, a, b, rows, gain_final.reshape(1, d))


def _prepare_weights(w_in, gdn_a_log, gdn_dt_bias, w_branch_a, w_branch_b, w_out, peer_w_q,
                     peer_sub_keys, peer_u, peer_v):
    o_z = GDN_QKV + GDN_WIDTH
    o_ab = o_z + 4 * GDN_HEADS
    o_b = o_ab + 2 * DIFF_QK + DIFF_WIDTH
    w = w_in[0]
    pad = jnp.zeros((D_MODEL, LANES - 4 * GDN_HEADS), F32)
    row = lambda p: jnp.concatenate([p[0].reshape(1, -1), jnp.zeros((1, LANES - 2 * GDN_HEADS), F32)], axis=1)
    return dict(
        w_a=jnp.concatenate([w[:, :o_z], w[:, o_b:]], axis=1).astype(BF16),
        w_ab=jnp.concatenate([w[:, o_z:o_ab], pad], axis=1).astype(BF16),
        w_b=w[:, o_ab:o_b].astype(BF16),
        alog=row(gdn_a_log), dtb=row(gdn_dt_bias),
        qscale=jnp.concatenate([jnp.full((1, DIFF_QK), DIFF_HD ** -0.5 * LOG2E, F32),
                                jnp.ones((1, DIFF_QK + DIFF_WIDTH), F32)], axis=1),
        w_branch_a=w_branch_a[0].astype(BF16), w_branch_b=w_branch_b[0].astype(BF16),
        w_out=w_out[0].astype(BF16), w_q=peer_w_q[0].astype(BF16),
        keys=peer_sub_keys[0].astype(BF16), u=peer_u[0].astype(BF16), v=peer_v[0].astype(BF16),
    )


def _trunk(x, pw, norm_mix_gain, conv_w, gdn_norm_gain, diff_lambda, diff_norm_gain,
           norm_ffn_gain, norm_final_gain, slopes):
    b, s, d = x.shape
    t = b * s
    xt = x.reshape(t, d)
    g_mix = norm_mix_gain[0]

    proj_a = norm_matmul(xt, g_mix, pw["w_a"], F32, name="in_proj_a")
    proj_b = norm_matmul(xt, g_mix, pw["w_b"], BF16, epilogue=_scale_epilogue,
                         extra=(pw["qscale"],), group=2 * DIFF_HD, name="in_proj_b")
    gb = norm_matmul(xt, g_mix, pw["w_ab"], F32, epilogue=_gate_epilogue,
                     extra=(pw["alog"], pw["dtb"]), name="in_proj_gates")

    qkv_n = conv_prep(proj_a.reshape(b, s, -1), conv_w[0])
    g = gb[:, :2 * GDN_HEADS].reshape(b, s, 2, GDN_HEADS)
    beta = gb[:, 2 * GDN_HEADS:4 * GDN_HEADS].reshape(b, s, 2, GDN_HEADS)
    gb_col = jnp.transpose(jnp.concatenate([g, beta], axis=-1), (0, 2, 1, 3))
    gb_col = jnp.pad(gb_col, ((0, 0), (0, 0), (0, 0), (0, LANES - 2 * GDN_HEADS)))
    g_row = jnp.transpose(g, (0, 2, 3, 1))
    o_gdn = gdn_scan(qkv_n, gb_col, g_row).reshape(2, t, GDN_WIDTH)

    o_b = diff_attention(proj_b.reshape(-1, b, s, 2 * DIFF_HD), slopes, diff_lambda[0], diff_norm_gain[0])
    o_b = o_b.reshape(t, DIFF_WIDTH)

    merged = merge_branches(o_gdn, proj_a, o_b, pw["w_branch_a"], pw["w_branch_b"], gdn_norm_gain[0])
    x1 = matmul_residual(merged, pw["w_out"], xt, name="out_proj")

    q = norm_matmul(x1, norm_ffn_gain[0], pw["w_q"], BF16, name="peer_query")
    a, bb, rows = peer_route(q, pw["keys"])
    y = peer_experts(x1, norm_ffn_gain[0], pw["u"], pw["v"], a, bb, rows, norm_final_gain)
    return y.reshape(b, s, d)


def kernel(x_prompt, x_sample, norm_mix_gain, w_in, conv_w, gdn_a_log, gdn_dt_bias, gdn_norm_gain,
           diff_lambda, diff_norm_gain, w_branch_a, w_branch_b, w_out, norm_ffn_gain,
           peer_w_q, peer_sub_keys, peer_u, peer_v, norm_final_gain):
    pw = _prepare_weights(w_in, gdn_a_log, gdn_dt_bias, w_branch_a, w_branch_b, w_out, peer_w_q,
                          peer_sub_keys, peer_u, peer_v)
    slopes = 2.0 ** (-8.0 * jnp.arange(1, DIFF_HEADS + 1, dtype=F32) / DIFF_HEADS)
    run = functools.partial(_trunk, pw=pw, norm_mix_gain=norm_mix_gain, conv_w=conv_w,
                            gdn_norm_gain=gdn_norm_gain, diff_lambda=diff_lambda,
                            diff_norm_gain=diff_norm_gain, norm_ffn_gain=norm_ffn_gain,
                            norm_final_gain=norm_final_gain, slopes=slopes)
    return (run(x_prompt), run(x_sample))
```

```python
import functools
import math

import jax
import jax.numpy as jnp
from jax import lax
from jax.experimental import pallas as pl
from jax.experimental.pallas import tpu as pltpu

F32 = jnp.float32
BF16 = jnp.bfloat16

D_MODEL = 2048
GDN_HEADS = 8
GDN_DK = 128
GDN_QK = 1024
GDN_WIDTH = 1024
GDN_QKV = 3072
GDN_CONV = 5
GDN_CHUNK = 128
DIFF_HEADS = 4
DIFF_HD = 128
DIFF_QK = 1024
DIFF_WIDTH = 1024
PEER_HEADS = 8
PEER_NKEYS = 128
PEER_EXPERTS = PEER_NKEYS * PEER_NKEYS
PEER_DQ = 256
PEER_TOPK = 16
NORM_EPS = 1e-6
LAM_INIT = 0.8 - 0.6 * math.exp(-0.3 * 0)

LANES = 128
SUBLANES = 8
VMEM_LIMIT_BYTES = 56 * 1024 * 1024
PEER_VMEM_LIMIT_BYTES = 60 * 1024 * 1024

NEG_INF = float("-inf")
LOG2E = math.log2(math.e)


def _params(*sem, vmem_limit=VMEM_LIMIT_BYTES):
    return pltpu.CompilerParams(dimension_semantics=sem, vmem_limit_bytes=vmem_limit)


def _tile(n, pref):
    t = min(n, pref)
    while n % t:
        t //= 2
    return t


def _dot(a, b):
    return jnp.dot(a, b, preferred_element_type=F32)


def _dot_nt(a, b):
    return lax.dot_general(a, b, (((1,), (1,)), ((), ())), preferred_element_type=F32)


def _dot_tn(a, b):
    return lax.dot_general(a, b, (((0,), (0,)), ((), ())), preferred_element_type=F32)


def _sigmoid(x):
    return 1.0 / (1.0 + jnp.exp(-x))


def _rms(x, gain):
    return x * lax.rsqrt(jnp.mean(x * x, axis=-1, keepdims=True) + NORM_EPS) * gain


def _norm_matmul_kernel(x_ref, g_ref, w_ref, *rest, epilogue, n_extra):
    extra, o_ref, h_ref = rest[:n_extra], rest[n_extra], rest[n_extra + 1]

    @pl.when(pl.program_id(1) == 0)
    def _():
        h_ref[...] = _rms(x_ref[...], g_ref[...]).astype(BF16)

    acc = _dot(h_ref[...], w_ref[...])
    if epilogue is not None:
        acc = epilogue(acc, *[e[...] for e in extra])
    if len(o_ref.shape) == 2:
        o_ref[...] = acc.astype(o_ref.dtype)
    else:
        gw = o_ref.shape[2]
        for gi in range(o_ref.shape[0]):
            o_ref[gi] = acc[:, gi * gw:(gi + 1) * gw].astype(o_ref.dtype)


def norm_matmul(x, gain, w, out_dtype, *, epilogue=None, extra=(), tm=1024, tn=1024, group=None, name):
    t, d = x.shape
    n = w.shape[1]
    tm, tn = _tile(t, tm), _tile(n, tn)
    kern = functools.partial(_norm_matmul_kernel, epilogue=epilogue, n_extra=len(extra))
    if group is None:
        out_spec = pl.BlockSpec((tm, tn), lambda i, j: (i, j))
        out_shape = jax.ShapeDtypeStruct((t, n), out_dtype)
    else:
        out_spec = pl.BlockSpec((tn // group, tm, group), lambda i, j: (j, i, 0))
        out_shape = jax.ShapeDtypeStruct((n // group, t, group), out_dtype)
    return pl.pallas_call(
        kern,
        grid=(t // tm, n // tn),
        in_specs=[pl.BlockSpec((tm, d), lambda i, j: (i, 0)),
                  pl.BlockSpec((1, d), lambda i, j: (0, 0)),
                  pl.BlockSpec((d, tn), lambda i, j: (0, j))]
                 + [pl.BlockSpec((1, tn), lambda i, j: (0, j)) for _ in extra],
        out_specs=out_spec,
        out_shape=out_shape,
        scratch_shapes=[pltpu.VMEM((tm, d), BF16)],
        compiler_params=_params("parallel", "arbitrary"),
        name=name,
    )(x, gain.reshape(1, d), w, *extra)


def _gate_epilogue(acc, alog, dtb):
    z = acc + dtb
    softplus = jnp.maximum(z, 0.0) + jnp.log(1.0 + jnp.exp(-jnp.abs(z)))
    g = -jnp.exp(alog) * softplus
    lane = lax.broadcasted_iota(jnp.int32, acc.shape, 1)
    return jnp.where(lane < 2 * GDN_HEADS, g, _sigmoid(acc))


def _scale_epilogue(acc, scale):
    return acc * scale


def _matmul_res_kernel(a_ref, w_ref, r_ref, o_ref):
    o_ref[...] = r_ref[...] + _dot(a_ref[...], w_ref[...])


def matmul_residual(a, w, res, *, tm=1024, tn=1024, name):
    t, k = a.shape
    n = w.shape[1]
    tm, tn = _tile(t, tm), _tile(n, tn)
    return pl.pallas_call(
        _matmul_res_kernel,
        grid=(t // tm, n // tn),
        in_specs=[pl.BlockSpec((tm, k), lambda i, j: (i, 0)),
                  pl.BlockSpec((k, tn), lambda i, j: (0, j)),
                  pl.BlockSpec((tm, tn), lambda i, j: (i, j))],
        out_specs=pl.BlockSpec((tm, tn), lambda i, j: (i, j)),
        out_shape=jax.ShapeDtypeStruct((t, n), F32),
        compiler_params=_params("parallel", "arbitrary"),
        name=name,
    )(a, w, res)


def _conv_kernel(cur_ref, prev_ref, next_ref, w_ref, o_ref, pad_ref):
    i, c = pl.program_id(1), pl.program_id(2)
    ts = cur_ref.shape[1]
    halo = SUBLANES
    pad_ref[0:halo, :] = jnp.where(i > 0, prev_ref[0], 0.0)
    pad_ref[halo:halo + ts, :] = cur_ref[0]
    pad_ref[halo + ts:2 * halo + ts, :] = jnp.where(i < pl.num_programs(1) - 1, next_ref[0], 0.0)
    w = w_ref[...]
    first = halo - (GDN_CONV - 1) // 2
    acc = w[0:1, :] * pad_ref[first:first + ts, :]
    for k in range(1, GDN_CONV):
        acc = acc + w[k:k + 1, :] * pad_ref[first + k:first + k + ts, :]
    y = acc * _sigmoid(acc)
    rs = lax.rsqrt(jnp.sum(y * y, axis=-1, keepdims=True) + 1e-6)
    f = jnp.where(c < GDN_HEADS, rs * (GDN_DK ** -0.5), jnp.where(c < 2 * GDN_HEADS, rs, 1.0))
    o_ref[0] = (y * f).astype(o_ref.dtype)


def conv_prep(proj_a, conv_w, *, ts=2048):
    b, s, _ = proj_a.shape
    ts = _tile(s, ts)
    r = ts // SUBLANES
    last = s // SUBLANES - 1
    return pl.pallas_call(
        _conv_kernel,
        grid=(b, s // ts, GDN_QKV // LANES),
        in_specs=[pl.BlockSpec((1, ts, LANES), lambda b_, i, c: (b_, i, c)),
                  pl.BlockSpec((1, SUBLANES, LANES), lambda b_, i, c: (b_, jnp.maximum(i * r - 1, 0), c)),
                  pl.BlockSpec((1, SUBLANES, LANES), lambda b_, i, c: (b_, jnp.minimum((i + 1) * r, last), c)),
                  pl.BlockSpec((GDN_CONV, LANES), lambda b_, i, c: (0, c))],
        out_specs=pl.BlockSpec((1, ts, LANES), lambda b_, i, c: (b_, i, c)),
        out_shape=jax.ShapeDtypeStruct((b, s, GDN_QKV), BF16),
        scratch_shapes=[pltpu.VMEM((ts + 2 * SUBLANES, LANES), F32)],
        compiler_params=_params("parallel", "parallel", "parallel"),
        name="gdn_conv_prep",
    )(proj_a, proj_a, proj_a, conv_w)


def _split3(x):
    hi = x.astype(BF16)
    r = x - hi.astype(F32)
    mid = r.astype(BF16)
    lo = (r - mid.astype(F32)).astype(BF16)
    return hi, mid, lo


def _gdn_kernel(q_ref, k_ref, v_ref, gb_ref, gr_ref, o_ref, state_ref):
    d, i = pl.program_id(1), pl.program_id(2)
    c = GDN_CHUNK

    @pl.when(i == 0)
    def _():
        state_ref[...] = jnp.zeros_like(state_ref)

    row = lax.broadcasted_iota(jnp.int32, (c, c), 0)
    col = lax.broadcasted_iota(jnp.int32, (c, c), 1)
    order = jnp.where(d == 0, row - col, col - row)
    incl = order >= 0
    strict = order > 0
    eye = jnp.where(row == col, 1.0, 0.0)
    ones_incl = jnp.where(incl, 1.0, 0.0).astype(BF16)

    gb = gb_ref[0, 0]
    g_col = gb[:, 0:GDN_HEADS]
    g_row = gr_ref[0, 0]
    gcum_col = sum(_dot(ones_incl, p) for p in _split3(g_col))
    gcum_row = sum(_dot_nt(p, ones_incl) for p in _split3(g_row))
    g_tot = jnp.sum(g_col, axis=0, keepdims=True)
    beta_all = gb[:, GDN_HEADS:2 * GDN_HEADS]

    heads = range(GDN_HEADS)
    hs = [slice(h * GDN_DK, (h + 1) * GDN_DK) for h in heads]
    q = [q_ref[0, :, hs[h]] for h in heads]
    k = [k_ref[0, :, hs[h]] for h in heads]
    beta = [beta_all[:, h:h + 1] for h in heads]
    gcc = [gcum_col[:, h:h + 1] for h in heads]
    gt = [g_tot[:, h:h + 1] for h in heads]
    kf = [k[h].astype(F32) for h in heads]
    kb = [kf[h] * beta[h] for h in heads]
    akk = [_dot_nt(kb[h].astype(BF16), k[h]) for h in heads]
    qk = [_dot_nt(q[h], k[h]) for h in heads]
    decay = [jnp.where(incl, jnp.exp(jnp.where(incl, gcc[h] - gcum_row[h:h + 1, :], 0.0)), 0.0)
             for h in heads]

    p = [-jnp.where(strict, akk[h] * decay[h], 0.0) for h in heads]
    t_inv = [eye + p[h] for h in heads]
    for _ in range(int(math.log2(c)) - 1):
        pb = [p[h].astype(BF16) for h in heads]
        p = [_dot(pb[h], pb[h]) for h in heads]
        t_inv = [t_inv[h] + _dot(t_inv[h].astype(BF16), p[h].astype(BF16)) for h in heads]

    eg = [jnp.exp(gcc[h]) for h in heads]
    rhs = [jnp.concatenate([(v_ref[0, :, hs[h]].astype(F32) * beta[h]).astype(BF16),
                            (kb[h] * eg[h]).astype(BF16)], axis=1) for h in heads]
    uw = [_dot(t_inv[h].astype(BF16), rhs[h]) for h in heads]
    lhs_o = [jnp.concatenate([(q[h].astype(F32) * eg[h]).astype(BF16),
                              jnp.where(incl, qk[h] * decay[h], 0.0).astype(BF16)], axis=1) for h in heads]
    k_tail = [(kf[h] * jnp.exp(gt[h] - gcc[h])).astype(BF16) for h in heads]
    s = [state_ref[h] for h in heads]
    sb = [s[h].astype(BF16) for h in heads]
    vb = [(uw[h][:, :GDN_DK] - _dot(uw[h][:, GDN_DK:].astype(BF16), sb[h])).astype(BF16) for h in heads]
    o = [_dot(lhs_o[h], jnp.concatenate([sb[h], vb[h]], axis=0)) for h in heads]
    s_new = [s[h] * jnp.exp(gt[h]) + _dot_tn(k_tail[h], vb[h]) for h in heads]
    for h in heads:
        state_ref[h] = s_new[h]
        o_ref[0, 0, :, hs[h]] = o[h]


def gdn_scan(qkv_n, gb_col, g_row):
    b, s, _ = qkv_n.shape
    c = GDN_CHUNK
    nc = s // c

    def cidx(d, i):
        return i + d * (nc - 1 - 2 * i)

    def qkv_spec(part):
        return pl.BlockSpec((1, c, GDN_QK), lambda b_, d, i: (b_, cidx(d, i), part))

    col_spec = pl.BlockSpec((1, 1, c, LANES), lambda b_, d, i: (b_, d, cidx(d, i), 0))
    row_spec = pl.BlockSpec((1, 1, GDN_HEADS, c), lambda b_, d, i: (b_, d, 0, cidx(d, i)))
    return pl.pallas_call(
        _gdn_kernel,
        grid=(b, 2, nc),
        in_specs=[qkv_spec(0), qkv_spec(1), qkv_spec(2), col_spec, row_spec],
        out_specs=pl.BlockSpec((1, 1, c, GDN_WIDTH), lambda b_, d, i: (d, b_, cidx(d, i), 0)),
        out_shape=jax.ShapeDtypeStruct((2, b, s, GDN_WIDTH), F32),
        scratch_shapes=[pltpu.VMEM((GDN_HEADS, GDN_DK, GDN_DK), F32)],
        compiler_params=_params("parallel", "parallel", "arbitrary"),
        name="gdn_scan",
    )(qkv_n, qkv_n, qkv_n, gb_col, g_row)


def _attn_kernel(slopes_ref, q_ref, k_ref, v_ref, lam_ref, gain_ref, o_ref, m_ref, l_ref, acc_ref, tile_ref):
    h, qi, sweep, ki = pl.program_id(1), pl.program_id(2), pl.program_id(3), pl.program_id(4)
    tq = tk = q_ref.shape[2]
    sub_blocks = k_ref.shape[2] // tk
    groups = tk // LANES
    c2 = slopes_ref[h] * LOG2E * jnp.ones((1, LANES), F32)

    @pl.when((sweep == 0) & (ki == 0))
    def _():
        m_ref[...] = jnp.full_like(m_ref, NEG_INF)
        rel = (lax.broadcasted_iota(jnp.int32, (tq, tk), 1)
               - lax.broadcasted_iota(jnp.int32, (tq, tk), 0)).astype(F32)
        t0 = c2[:, 0:1] * rel
        tile_ref[0] = t0
        tile_ref[1] = -t0
        tile_ref[2] = -jnp.abs(t0)

    @pl.when((sweep == 1) & (ki == 0))
    def _():
        l_ref[...] = jnp.zeros_like(l_ref)
        acc_ref[...] = jnp.zeros_like(acc_ref)
        for c in range(2):
            m_ref[c] = jnp.broadcast_to(jnp.max(m_ref[c], axis=-1, keepdims=True), (tq, LANES))

    def block_bias(j):
        kj = ki * sub_blocks + j
        case = jnp.where(kj < qi, 0, jnp.where(kj > qi, 1, 2))
        gap = jnp.abs(qi - kj) * tq * jnp.ones((1, LANES), jnp.int32)
        return case, -c2 * gap.astype(F32)

    def scores(c, j, case):
        cs = slice(c * DIFF_HD, (c + 1) * DIFF_HD)
        return _dot_nt(q_ref[0, 0, :, cs], k_ref[0, 0, j * tk:(j + 1) * tk, cs]) + tile_ref[case]

    def lane_groups(x):
        return [x[:, g * LANES:(g + 1) * LANES] for g in range(groups)]

    @pl.when(sweep == 0)
    def _():
        m = [m_ref[c] for c in range(2)]
        for j in range(sub_blocks):
            case, cb = block_bias(j)
            for c in range(2):
                m[c] = jnp.maximum(m[c], functools.reduce(jnp.maximum, lane_groups(scores(c, j, case))) + cb)
        for c in range(2):
            m_ref[c] = m[c]

    @pl.when(sweep == 1)
    def _():
        m = [m_ref[c] for c in range(2)]
        l = [l_ref[c] for c in range(2)]
        pv = [None, None]
        for j in range(sub_blocks):
            case, cb = block_bias(j)
            v = v_ref[0, 0, j * tk:(j + 1) * tk, :]
            for c in range(2):
                shift = jnp.concatenate([cb - m[c]] * groups, axis=1)
                p = jnp.exp2(scores(c, j, case) + shift)
                l[c] = l[c] + functools.reduce(jnp.add, lane_groups(p))
                part = _dot(p.astype(BF16), v)
                pv[c] = part if pv[c] is None else pv[c] + part
        for c in range(2):
            l_ref[c] = l[c]
            acc_ref[c] += pv[c]

    @pl.when((sweep == 1) & (ki == pl.num_programs(4) - 1))
    def _():
        lp = lam_ref[...]
        lam = (jnp.exp(jnp.sum(lp[0:1] * lp[1:2], axis=-1, keepdims=True))
               - jnp.exp(jnp.sum(lp[2:3] * lp[3:4], axis=-1, keepdims=True)) + LAM_INIT)
        l0 = jnp.sum(l_ref[0], axis=-1, keepdims=True)
        l1 = jnp.sum(l_ref[1], axis=-1, keepdims=True)
        o = acc_ref[0] / l0 - lam * (acc_ref[1] / l1)
        o_ref[0] = (_rms(o, gain_ref[...]) * (1.0 - LAM_INIT)).astype(o_ref.dtype)


def diff_attention(proj_b, slopes, diff_lambda, gain, *, tile=512, kv_blocks=16):
    _, b, s, e = proj_b.shape
    tq = tk = _tile(s, tile)
    tkv = _tile(s, kv_blocks * tk)
    return pl.pallas_call(
        _attn_kernel,
        grid=(b, DIFF_HEADS, s // tq, 2, s // tkv),
        in_specs=[pl.BlockSpec(memory_space=pltpu.SMEM),
                  pl.BlockSpec((1, 1, tq, e), lambda b_, h, qi, sw, ki: (h, b_, qi, 0)),
                  pl.BlockSpec((1, 1, tkv, e), lambda b_, h, qi, sw, ki: (DIFF_HEADS + h, b_, ki, 0)),
                  pl.BlockSpec((1, 1, tkv, e), lambda b_, h, qi, sw, ki: (2 * DIFF_HEADS + h, b_, ki * sw, 0)),
                  pl.BlockSpec((4, DIFF_HD), lambda b_, h, qi, sw, ki: (0, 0)),
                  pl.BlockSpec((1, e), lambda b_, h, qi, sw, ki: (0, 0))],
        out_specs=pl.BlockSpec((1, tq, e), lambda b_, h, qi, sw, ki: (b_, qi, h)),
        out_shape=jax.ShapeDtypeStruct((b, s, DIFF_WIDTH), BF16),
        scratch_shapes=[pltpu.VMEM((2, tq, LANES), F32),
                        pltpu.VMEM((2, tq, LANES), F32),
                        pltpu.VMEM((2, tq, e), F32),
                        pltpu.VMEM((3, tq, tk), F32)],
        compiler_params=_params("parallel", "parallel", "parallel", "arbitrary", "arbitrary"),
        name="diff_attention",
    )(slopes, proj_b, proj_b, proj_b, diff_lambda, gain.reshape(1, e))


def _merge_kernel(o_ref, z_ref, ga_ref, gb_ref, ob_ref, wa_ref, wb_ref, gain_ref, out_ref, oa_ref):
    @pl.when(pl.program_id(1) == 0)
    def _():
        gain = gain_ref[...]
        for h in range(GDN_HEADS):
            hs = slice(h * GDN_DK, (h + 1) * GDN_DK)
            z = z_ref[:, hs]
            oa_ref[:, hs] = (_rms(o_ref[0, :, hs] + o_ref[1, :, hs], gain) * (z * _sigmoid(z))).astype(BF16)

    out_ref[...] = (_sigmoid(ga_ref[...]) * _dot(oa_ref[...], wa_ref[...])
                    + _sigmoid(gb_ref[...]) * _dot(ob_ref[...], wb_ref[...])).astype(out_ref.dtype)


def merge_branches(o_gdn, proj_a, o_b, w_a, w_b, gain, *, tm=256, tn=2048):
    t = o_b.shape[0]
    tm = _tile(t, tm)
    z_blk = GDN_QKV // GDN_WIDTH
    ga_blk = (GDN_QKV + GDN_WIDTH) // tn
    gb_blk = (GDN_QKV + GDN_WIDTH + D_MODEL) // tn
    return pl.pallas_call(
        _merge_kernel,
        grid=(t // tm, D_MODEL // tn),
        in_specs=[pl.BlockSpec((2, tm, GDN_WIDTH), lambda i, j: (0, i, 0)),
                  pl.BlockSpec((tm, GDN_WIDTH), lambda i, j: (i, z_blk)),
                  pl.BlockSpec((tm, tn), lambda i, j: (i, ga_blk + j)),
                  pl.BlockSpec((tm, tn), lambda i, j: (i, gb_blk + j)),
                  pl.BlockSpec((tm, DIFF_WIDTH), lambda i, j: (i, 0)),
                  pl.BlockSpec((GDN_WIDTH, tn), lambda i, j: (0, j)),
                  pl.BlockSpec((DIFF_WIDTH, tn), lambda i, j: (0, j)),
                  pl.BlockSpec((1, GDN_DK), lambda i, j: (0, 0))],
        out_specs=pl.BlockSpec((tm, tn), lambda i, j: (i, j)),
        out_shape=jax.ShapeDtypeStruct((t, D_MODEL), BF16),
        scratch_shapes=[pltpu.VMEM((tm, GDN_WIDTH), BF16)],
        compiler_params=_params("parallel", "arbitrary"),
        name="merge_branches",
    )(o_gdn, proj_a, proj_a, proj_a, o_b, w_a, w_b, gain.reshape(1, GDN_DK))


def _extract_distinct(x, n):
    vals, cnts = [], []
    for _ in range(n):
        m = jnp.max(x, axis=0, keepdims=True)
        eq = x == m
        cnt = jnp.sum(jnp.where(eq, 1.0, 0.0), axis=0, keepdims=True)
        vals.append(m)
        cnts.append(jnp.where(m == NEG_INF, 0.0, cnt))
        x = jnp.where(eq, NEG_INF, x)
    return vals, cnts


def _pair_sums(at, bt):
    s8 = SUBLANES
    return jnp.concatenate([at[k:k + 1] + bt[0:s8] for k in range(s8)]
                           + [at[0:1] + bt[s8:], at[s8:] + bt[0:1]], axis=0)


def _kth_largest(x, mult, k):
    tau = jnp.zeros_like(x[0:1])
    cum = jnp.zeros_like(x[0:1])
    for _ in range(k):
        m = jnp.max(x, axis=0, keepdims=True)
        eq = x == m
        tau = jnp.where(cum < k, m, tau)
        cum = cum + jnp.sum(jnp.where(eq, mult, 0.0), axis=0, keepdims=True)
        x = jnp.where(eq, NEG_INF, x)
    return tau


def _route_kernel(q_ref, keys_ref, a_ref, b_ref, rows_ref):
    half = PEER_DQ // 2
    s1 = _dot_nt(keys_ref[0, 0], q_ref[:, 0:half])
    s2 = _dot_nt(keys_ref[0, 1], q_ref[:, half:PEER_DQ])
    v1, n1 = _extract_distinct(s1, PEER_TOPK)
    v2, n2 = _extract_distinct(s2, PEER_TOPK)
    at = jnp.concatenate([(v - v1[0]) * LOG2E for v in v1], axis=0)
    bt = jnp.concatenate([(v - v2[0]) * LOG2E for v in v2], axis=0)
    s8 = SUBLANES
    na = jnp.concatenate(n1, axis=0)
    nb = jnp.concatenate(n2, axis=0)
    mult = jnp.concatenate([na[k:k + 1] * nb[0:s8] for k in range(s8)]
                           + [na[0:1] * nb[s8:], na[s8:] * nb[0:1]], axis=0)
    cand = _pair_sums(at, bt)
    tau = _kth_largest(cand, mult, PEER_TOPK)
    z = jnp.sum(jnp.where(cand >= tau, mult * jnp.exp2(cand), 0.0), axis=0, keepdims=True)
    nlz = -jnp.log2(z)
    a_ref[0] = (s1 - v1[0]) * LOG2E
    b_ref[0] = (s2 - v2[0]) * LOG2E + nlz
    tau_z = jnp.min(jnp.where(cand >= tau, _pair_sums(at, bt + nlz), jnp.inf), axis=0, keepdims=True)
    rows_ref[0] = jnp.concatenate([tau_z, jnp.zeros((s8 - 1, tau_z.shape[1]), F32)], axis=0)


def peer_route(q, keys, *, tt=512):
    t = q.shape[0]
    tt = _tile(t, tt)
    tab = jax.ShapeDtypeStruct((PEER_HEADS, PEER_NKEYS, t), F32)
    tab_spec = pl.BlockSpec((1, PEER_NKEYS, tt), lambda i, p: (p, 0, i))
    return pl.pallas_call(
        _route_kernel,
        grid=(t // tt, PEER_HEADS),
        in_specs=[pl.BlockSpec((tt, PEER_DQ), lambda i, p: (i, p)),
                  pl.BlockSpec((1, 2, PEER_NKEYS, PEER_DQ // 2), lambda i, p: (p, 0, 0, 0))],
        out_specs=[tab_spec, tab_spec, pl.BlockSpec((1, SUBLANES, tt), lambda i, p: (p, 0, i))],
        out_shape=[tab, tab, jax.ShapeDtypeStruct((PEER_HEADS, SUBLANES, t), F32)],
        compiler_params=_params("parallel", "parallel"),
        name="peer_route",
    )(q, keys)


def _gelu_tanh(x):
    return x * (0.5 * (1.0 + jnp.tanh(math.sqrt(2.0 / math.pi) * (x + 0.044715 * (x * x * x)))))


def _peer_kernel(x_ref, gf_ref, u_ref, v_ref, a_ref, b_ref, rows_ref, gfin_ref,
                 o_ref, h_ref, gate_ref, coef_ref):
    e = pl.program_id(1)
    te, tt = u_ref.shape[0], x_ref.shape[0]
    rows = te // PEER_NKEYS

    @pl.when(e == 0)
    def _():
        h_ref[...] = _rms(x_ref[...], gf_ref[...]).astype(BF16)
        o_ref[...] = jnp.zeros_like(o_ref)

    for il in range(rows):
        rs = slice(il * PEER_NKEYS, (il + 1) * PEER_NKEYS)
        a_rows = [a_ref[p, pl.ds(e * rows + il, 1), :] for p in range(PEER_HEADS)]
        for lb in range(tt // LANES):
            ls = slice(lb * LANES, (lb + 1) * LANES)
            g = None
            for p in range(PEER_HEADS):
                pair = a_rows[p][:, ls] + b_ref[p, :, ls]
                w = jnp.where(pair >= rows_ref[p, 0:1, ls], jnp.exp2(pair), 0.0)
                g = w if g is None else g + w
            gate_ref[rs, ls] = g
    act =_gelu_tanh(_dot_nt(u_ref[...], h_ref[...]))
    coef_ref[...] = (act * gate_ref[...]).astype(BF16)
    o_ref[...] += _dot_tn(coef_ref[...], v_ref[...])

    @pl.when(e == pl.num_programs(1) - 1)
    def _():
        o_ref[...] = _rms(x_ref[...] + o_ref[...], gfin_ref[...])


def peer_experts(x1, gain_ffn, u, v, a, b, rows, gain_final, *, tt=1024, te=512):
    t, d = x1.shape
    tt, te = _tile(t, tt), _tile(PEER_EXPERTS, te)
    once = pl.Buffered(1)
    tab_spec = pl.BlockSpec((PEER_HEADS, PEER_NKEYS, tt), lambda i, e: (0, 0, i))
    return pl.pallas_call(
        _peer_kernel,
        grid=(t // tt, PEER_EXPERTS // te),
        in_specs=[pl.BlockSpec((tt, d), lambda i, e: (i, 0), pipeline_mode=once),
                  pl.BlockSpec((1, d), lambda i, e: (0, 0)),
                  pl.BlockSpec((te, d), lambda i, e: (e, 0)),
                  pl.BlockSpec((te, d), lambda i, e: (e, 0)),
                  tab_spec, tab_spec,
                  pl.BlockSpec((PEER_HEADS, SUBLANES, tt), lambda i, e: (0, 0, i)),
                  pl.BlockSpec((1, d), lambda i, e: (0, 0))],
        out_specs=pl.BlockSpec((tt, d), lambda i, e: (i, 0), pipeline_mode=once),
        out_shape=jax.ShapeDtypeStruct((t, d), F32),
        scratch_shapes=[pltpu.VMEM((tt, d), BF16),
                        pltpu.VMEM((te, tt), F32),
                        pltpu.VMEM((te, tt), BF16)],
        compiler_params=_params("parallel", "arbitrary", vmem_limit=PEER_VMEM_LIMIT_BYTES),
        name="peer_experts",
    )(x1, gain_ffn.reshape(1, d), u, v, a, b, rows, gain_final.reshape(1, d))


def _prepare_weights(w_in, gdn_a_log, gdn_dt_bias, w_branch_a, w_branch_b, w_out, peer_w_q,
                     peer_sub_keys, peer_u, peer_v):
    o_z = GDN_QKV + GDN_WIDTH
    o_ab = o_z + 4 * GDN_HEADS
    o_b = o_ab + 2 * DIFF_QK + DIFF_WIDTH
    w = w_in[0]
    pad = jnp.zeros((D_MODEL, LANES - 4 * GDN_HEADS), F32)
    row = lambda p: jnp.concatenate([p[0].reshape(1, -1), jnp.zeros((1, LANES - 2 * GDN_HEADS), F32)], axis=1)
    return dict(
        w_a=jnp.concatenate([w[:, :o_z], w[:, o_b:]], axis=1).astype(BF16),
        w_ab=jnp.concatenate([w[:, o_z:o_ab], pad], axis=1).astype(BF16),
        w_b=w[:, o_ab:o_b].astype(BF16),
        alog=row(gdn_a_log), dtb=row(gdn_dt_bias),
        qscale=jnp.concatenate([jnp.full((1, DIFF_QK), DIFF_HD ** -0.5 * LOG2E, F32),
                                jnp.ones((1, DIFF_QK + DIFF_WIDTH), F32)], axis=1),
        w_branch_a=w_branch_a[0].astype(BF16), w_branch_b=w_branch_b[0].astype(BF16),
        w_out=w_out[0].astype(BF16), w_q=peer_w_q[0].astype(BF16),
        keys=peer_sub_keys[0].astype(BF16), u=peer_u[0].astype(BF16), v=peer_v[0].astype(BF16),
    )


def _trunk(x, pw, norm_mix_gain, conv_w, gdn_norm_gain, diff_lambda, diff_norm_gain,
           norm_ffn_gain, norm_final_gain, slopes):
    b, s, d = x.shape
    t = b * s
    xt = x.reshape(t, d)
    g_mix = norm_mix_gain[0]

    proj_a = norm_matmul(xt, g_mix, pw["w_a"], F32, name="in_proj_a")
    proj_b = norm_matmul(xt, g_mix, pw["w_b"], BF16, epilogue=_scale_epilogue,
                         extra=(pw["qscale"],), group=2 * DIFF_HD, name="in_proj_b")
    gb = norm_matmul(xt, g_mix, pw["w_ab"], F32, epilogue=_gate_epilogue,
                     extra=(pw["alog"], pw["dtb"]), name="in_proj_gates")

    qkv_n = conv_prep(proj_a.reshape(b, s, -1), conv_w[0])
    g = gb[:, :2 * GDN_HEADS].reshape(b, s, 2, GDN_HEADS)
    beta = gb[:, 2 * GDN_HEADS:4 * GDN_HEADS].reshape(b, s, 2, GDN_HEADS)
    gb_col = jnp.transpose(jnp.concatenate([g, beta], axis=-1), (0, 2, 1, 3))
    gb_col = jnp.pad(gb_col, ((0, 0), (0, 0), (0, 0), (0, LANES - 2 * GDN_HEADS)))
    g_row = jnp.transpose(g, (0, 2, 3, 1))
    o_gdn = gdn_scan(qkv_n, gb_col, g_row).reshape(2, t, GDN_WIDTH)

    o_b = diff_attention(proj_b.reshape(-1, b, s, 2 * DIFF_HD), slopes, diff_lambda[0], diff_norm_gain[0])
    o_b = o_b.reshape(t, DIFF_WIDTH)

    merged = merge_branches(o_gdn, proj_a, o_b, pw["w_branch_a"], pw["w_branch_b"], gdn_norm_gain[0])
    x1 = matmul_residual(merged, pw["w_out"], xt, name="out_proj")

    q = norm_matmul(x1, norm_ffn_gain[0], pw["w_q"], BF16, name="peer_query")
    a, bb, rows = peer_route(q, pw["keys"])
    y = peer_experts(x1, norm_ffn_gain[0], pw["u"], pw["v"], a, bb, rows, norm_final_gain)
    return y.reshape(b, s, d)


def kernel(x_prompt, x_sample, norm_mix_gain, w_in, conv_w, gdn_a_log, gdn_dt_bias, gdn_norm_gain,
           diff_lambda, diff_norm_gain, w_branch_a, w_branch_b, w_out, norm_ffn_gain,
           peer_w_q, peer_sub_keys, peer_u, peer_v, norm_final_gain):
    pw = _prepare_weights(w_in, gdn_a_log, gdn_dt_bias, w_branch_a, w_branch_b, w_out, peer_w_q,
                          peer_sub_keys, peer_u, peer_v)
    slopes = 2.0 ** (-8.0 * jnp.arange(1, DIFF_HEADS + 1, dtype=F32) / DIFF_HEADS)
    run = functools.partial(_trunk, pw=pw, norm_mix_gain=norm_mix_gain, conv_w=conv_w,
                            gdn_norm_gain=gdn_norm_gain, diff_lambda=diff_lambda,
                            diff_norm_gain=diff_norm_gain, norm_ffn_gain=norm_ffn_gain,
                            norm_final_gain=norm_final_gain, slopes=slopes)
    return (run(x_prompt), run(x_sample))
```

```python
import functools
import math

import jax
import jax.numpy as jnp
from jax import lax
from jax.experimental import pallas as pl
from jax.experimental.pallas import tpu as pltpu

F32 = jnp.float32
BF16 = jnp.bfloat16

D_MODEL = 2048
GDN_HEADS = 8
GDN_DK = 128
GDN_QK = 1024
GDN_WIDTH = 1024
GDN_QKV = 3072
GDN_CONV = 5
GDN_CHUNK = 128
DIFF_HEADS = 4
DIFF_HD = 128
DIFF_QK = 1024
DIFF_WIDTH = 1024
PEER_HEADS = 8
PEER_NKEYS = 128
PEER_EXPERTS = PEER_NKEYS * PEER_NKEYS
PEER_DQ = 256
PEER_TOPK = 16
NORM_EPS = 1e-6
LAM_INIT = 0.8 - 0.6 * math.exp(-0.3 * 0)

LANES = 128
SUBLANES = 8
VMEM_LIMIT_BYTES = 56 * 1024 * 1024
PEER_VMEM_LIMIT_BYTES = 60 * 1024 * 1024

NEG_INF = float("-inf")
LOG2E = math.log2(math.e)


def _params(*sem, vmem_limit=VMEM_LIMIT_BYTES):
    return pltpu.CompilerParams(dimension_semantics=sem, vmem_limit_bytes=vmem_limit)


def _tile(n, pref):
    t = min(n, pref)
    while n % t:
        t //= 2
    return t


def _dot(a, b):
    return jnp.dot(a, b, preferred_element_type=F32)


def _dot_nt(a, b):
    return lax.dot_general(a, b, (((1,), (1,)), ((), ())), preferred_element_type=F32)


def _dot_tn(a, b):
    return lax.dot_general(a, b, (((0,), (0,)), ((), ())), preferred_element_type=F32)


def _sigmoid(x):
    return 1.0 / (1.0 + jnp.exp(-x))


def _rms(x, gain):
    return x * lax.rsqrt(jnp.mean(x * x, axis=-1, keepdims=True) + NORM_EPS) * gain


def _norm_matmul_kernel(x_ref, g_ref, w_ref, *rest, epilogue, n_extra):
    extra, o_ref, h_ref = rest[:n_extra], rest[n_extra], rest[n_extra + 1]

    @pl.when(pl.program_id(1) == 0)
    def _():
        h_ref[...] = _rms(x_ref[...], g_ref[...]).astype(BF16)

    acc = _dot(h_ref[...], w_ref[...])
    if epilogue is not None:
        acc = epilogue(acc, *[e[...] for e in extra])
    if len(o_ref.shape) == 2:
        o_ref[...] = acc.astype(o_ref.dtype)
    else:
        gw = o_ref.shape[2]
        for gi in range(o_ref.shape[0]):
            o_ref[gi] = acc[:, gi * gw:(gi + 1) * gw].astype(o_ref.dtype)


def norm_matmul(x, gain, w, out_dtype, *, epilogue=None, extra=(), tm=1024, tn=1024, group=None, name):
    t, d = x.shape
    n = w.shape[1]
    tm, tn = _tile(t, tm), _tile(n, tn)
    kern = functools.partial(_norm_matmul_kernel, epilogue=epilogue, n_extra=len(extra))
    if group is None:
        out_spec = pl.BlockSpec((tm, tn), lambda i, j: (i, j))
        out_shape = jax.ShapeDtypeStruct((t, n), out_dtype)
    else:
        out_spec = pl.BlockSpec((tn // group, tm, group), lambda i, j: (j, i, 0))
        out_shape = jax.ShapeDtypeStruct((n // group, t, group), out_dtype)
    return pl.pallas_call(
        kern,
        grid=(t // tm, n // tn),
        in_specs=[pl.BlockSpec((tm, d), lambda i, j: (i, 0)),
                  pl.BlockSpec((1, d), lambda i, j: (0, 0)),
                  pl.BlockSpec((d, tn), lambda i, j: (0, j))]
                 + [pl.BlockSpec((1, tn), lambda i, j: (0, j)) for _ in extra],
        out_specs=out_spec,
        out_shape=out_shape,
        scratch_shapes=[pltpu.VMEM((tm, d), BF16)],
        compiler_params=_params("parallel", "arbitrary"),
        name=name,
    )(x, gain.reshape(1, d), w, *extra)


def _gate_epilogue(acc, alog, dtb):
    z = acc + dtb
    softplus = jnp.maximum(z, 0.0) + jnp.log(1.0 + jnp.exp(-jnp.abs(z)))
    g = -jnp.exp(alog) * softplus
    lane = lax.broadcasted_iota(jnp.int32, acc.shape, 1)
    return jnp.where(lane < 2 * GDN_HEADS, g, _sigmoid(acc))


def _scale_epilogue(acc, scale):
    return acc * scale


def _matmul_res_kernel(a_ref, w_ref, r_ref, o_ref):
    o_ref[...] = r_ref[...] + _dot(a_ref[...], w_ref[...])


def matmul_residual(a, w, res, *, tm=1024, tn=1024, name):
    t, k = a.shape
    n = w.shape[1]
    tm, tn = _tile(t, tm), _tile(n, tn)
    return pl.pallas_call(
        _matmul_res_kernel,
        grid=(t // tm, n // tn),
        in_specs=[pl.BlockSpec((tm, k), lambda i, j: (i, 0)),
                  pl.BlockSpec((k, tn), lambda i, j: (0, j)),
                  pl.BlockSpec((tm, tn), lambda i, j: (i, j))],
        out_specs=pl.BlockSpec((tm, tn), lambda i, j: (i, j)),
        out_shape=jax.ShapeDtypeStruct((t, n), F32),
        compiler_params=_params("parallel", "arbitrary"),
        name=name,
    )(a, w, res)


def _conv_kernel(cur_ref, prev_ref, next_ref, w_ref, o_ref, pad_ref):
    i, c = pl.program_id(1), pl.program_id(2)
    ts = cur_ref.shape[1]
    halo = SUBLANES
    pad_ref[0:halo, :] = jnp.where(i > 0, prev_ref[0], 0.0)
    pad_ref[halo:halo + ts, :] = cur_ref[0]
    pad_ref[halo + ts:2 * halo + ts, :] = jnp.where(i < pl.num_programs(1) - 1, next_ref[0], 0.0)
    w = w_ref[...]
    first = halo - (GDN_CONV - 1) // 2
    acc = w[0:1, :] * pad_ref[first:first + ts, :]
    for k in range(1, GDN_CONV):
        acc = acc + w[k:k + 1, :] * pad_ref[first + k:first + k + ts, :]
    y = acc * _sigmoid(acc)
    rs = lax.rsqrt(jnp.sum(y * y, axis=-1, keepdims=True) + 1e-6)
    f = jnp.where(c < GDN_HEADS, rs * (GDN_DK ** -0.5), jnp.where(c < 2 * GDN_HEADS, rs, 1.0))
    o_ref[0] = (y * f).astype(o_ref.dtype)


def conv_prep(proj_a, conv_w, *, ts=2048):
    b, s, _ = proj_a.shape
    ts = _tile(s, ts)
    r = ts // SUBLANES
    last = s // SUBLANES - 1
    return pl.pallas_call(
        _conv_kernel,
        grid=(b, s // ts, GDN_QKV // LANES),
        in_specs=[pl.BlockSpec((1, ts, LANES), lambda b_, i, c: (b_, i, c)),
                  pl.BlockSpec((1, SUBLANES, LANES), lambda b_, i, c: (b_, jnp.maximum(i * r - 1, 0), c)),
                  pl.BlockSpec((1, SUBLANES, LANES), lambda b_, i, c: (b_, jnp.minimum((i + 1) * r, last), c)),
                  pl.BlockSpec((GDN_CONV, LANES), lambda b_, i, c: (0, c))],
        out_specs=pl.BlockSpec((1, ts, LANES), lambda b_, i, c: (b_, i, c)),
        out_shape=jax.ShapeDtypeStruct((b, s, GDN_QKV), BF16),
        scratch_shapes=[pltpu.VMEM((ts + 2 * SUBLANES, LANES), F32)],
        compiler_params=_params("parallel", "parallel", "parallel"),
        name="gdn_conv_prep",
    )(proj_a, proj_a, proj_a, conv_w)


def _split3(x):
    hi = x.astype(BF16)
    r = x - hi.astype(F32)
    mid = r.astype(BF16)
    lo = (r - mid.astype(F32)).astype(BF16)
    return hi, mid, lo


def _gdn_kernel(q_ref, k_ref, v_ref, gb_ref, gr_ref, o_ref, state_ref):
    d, i = pl.program_id(1), pl.program_id(2)
    c = GDN_CHUNK

    @pl.when(i == 0)
    def _():
        state_ref[...] = jnp.zeros_like(state_ref)

    row = lax.broadcasted_iota(jnp.int32, (c, c), 0)
    col = lax.broadcasted_iota(jnp.int32, (c, c), 1)
    order = jnp.where(d == 0, row - col, col - row)
    incl = order >= 0
    strict = order > 0
    eye = jnp.where(row == col, 1.0, 0.0)
    ones_incl = jnp.where(incl, 1.0, 0.0).astype(BF16)

    gb = gb_ref[0, 0]
    g_col = gb[:, 0:GDN_HEADS]
    g_row = gr_ref[0, 0]
    gcum_col = sum(_dot(ones_incl, p) for p in _split3(g_col))
    gcum_row = sum(_dot_nt(p, ones_incl) for p in _split3(g_row))
    g_tot = jnp.sum(g_col, axis=0, keepdims=True)
    beta_all = gb[:, GDN_HEADS:2 * GDN_HEADS]

    heads = range(GDN_HEADS)
    hs = [slice(h * GDN_DK, (h + 1) * GDN_DK) for h in heads]
    q = [q_ref[0, :, hs[h]] for h in heads]
    k = [k_ref[0, :, hs[h]] for h in heads]
    beta = [beta_all[:, h:h + 1] for h in heads]
    gcc = [gcum_col[:, h:h + 1] for h in heads]
    gt = [g_tot[:, h:h + 1] for h in heads]
    kf = [k[h].astype(F32) for h in heads]
    kb = [kf[h] * beta[h] for h in heads]
    akk = [_dot_nt(kb[h].astype(BF16), k[h]) for h in heads]
    qk = [_dot_nt(q[h], k[h]) for h in heads]
    decay = [jnp.where(incl, jnp.exp(jnp.where(incl, gcc[h] - gcum_row[h:h + 1, :], 0.0)), 0.0)
             for h in heads]

    p = [-jnp.where(strict, akk[h] * decay[h], 0.0) for h in heads]
    t_inv = [eye + p[h] for h in heads]
    for _ in range(int(math.log2(c)) - 1):
        pb = [p[h].astype(BF16) for h in heads]
        p = [_dot(pb[h], pb[h]) for h in heads]
        t_inv = [t_inv[h] + _dot(t_inv[h].astype(BF16), p[h].astype(BF16)) for h in heads]

    eg = [jnp.exp(gcc[h]) for h in heads]
    rhs = [jnp.concatenate([(v_ref[0, :, hs[h]].astype(F32) * beta[h]).astype(BF16),
                            (kb[h] * eg[h]).astype(BF16)], axis=1) for h in heads]
    uw = [_dot(t_inv[h].astype(BF16), rhs[h]) for h in heads]
    lhs_o = [jnp.concatenate([(q[h].astype(F32) * eg[h]).astype(BF16),
                              jnp.where(incl, qk[h] * decay[h], 0.0).astype(BF16)], axis=1) for h in heads]
    k_tail = [(kf[h] * jnp.exp(gt[h] - gcc[h])).astype(BF16) for h in heads]
    s = [state_ref[h] for h in heads]
    sb = [s[h].astype(BF16) for h in heads]
    vb = [(uw[h][:, :GDN_DK] - _dot(uw[h][:, GDN_DK:].astype(BF16), sb[h])).astype(BF16) for h in heads]
    o = [_dot(lhs_o[h], jnp.concatenate([sb[h], vb[h]], axis=0)) for h in heads]
    s_new = [s[h] * jnp.exp(gt[h]) + _dot_tn(k_tail[h], vb[h]) for h in heads]
    for h in heads:
        state_ref[h] = s_new[h]
        o_ref[0, 0, :, hs[h]] = o[h]


def gdn_scan(qkv_n, gb_col, g_row):
    b, s, _ = qkv_n.shape
    c = GDN_CHUNK
    nc = s // c

    def cidx(d, i):
        return i + d * (nc - 1 - 2 * i)

    def qkv_spec(part):
        return pl.BlockSpec((1, c, GDN_QK), lambda b_, d, i: (b_, cidx(d, i), part))

    col_spec = pl.BlockSpec((1, 1, c, LANES), lambda b_, d, i: (b_, d, cidx(d, i), 0))
    row_spec = pl.BlockSpec((1, 1, GDN_HEADS, c), lambda b_, d, i: (b_, d, 0, cidx(d, i)))
    return pl.pallas_call(
        _gdn_kernel,
        grid=(b, 2, nc),
        in_specs=[qkv_spec(0), qkv_spec(1), qkv_spec(2), col_spec, row_spec],
        out_specs=pl.BlockSpec((1, 1, c, GDN_WIDTH), lambda b_, d, i: (d, b_, cidx(d, i), 0)),
        out_shape=jax.ShapeDtypeStruct((2, b, s, GDN_WIDTH), F32),
        scratch_shapes=[pltpu.VMEM((GDN_HEADS, GDN_DK, GDN_DK), F32)],
        compiler_params=_params("parallel", "parallel", "arbitrary"),
        name="gdn_scan",
    )(qkv_n, qkv_n, qkv_n, gb_col, g_row)


def _attn_kernel(slopes_ref, q_ref, k_ref, v_ref, lam_ref, gain_ref, o_ref, m_ref, l_ref, acc_ref, tile_ref):
    h, qi, sweep, ki = pl.program_id(1), pl.program_id(2), pl.program_id(3), pl.program_id(4)
    tq = tk = q_ref.shape[2]
    sub_blocks = k_ref.shape[2] // tk
    groups = tk // LANES
    c2 = slopes_ref[h] * LOG2E * jnp.ones((1, LANES), F32)

    @pl.when((sweep == 0) & (ki == 0))
    def _():
        m_ref[...] = jnp.full_like(m_ref, NEG_INF)
        rel = (lax.broadcasted_iota(jnp.int32, (tq, tk), 1)
               - lax.broadcasted_iota(jnp.int32, (tq, tk), 0)).astype(F32)
        t0 = c2[:, 0:1] * rel
        tile_ref[0] = t0
        tile_ref[1] = -t0
        tile_ref[2] = -jnp.abs(t0)

    @pl.when((sweep == 1) & (ki == 0))
    def _():
        l_ref[...] = jnp.zeros_like(l_ref)
        acc_ref[...] = jnp.zeros_like(acc_ref)
        for c in range(2):
            m_ref[c] = jnp.broadcast_to(jnp.max(m_ref[c], axis=-1, keepdims=True), (tq, LANES))

    def block_bias(j):
        kj = ki * sub_blocks + j
        case = jnp.where(kj < qi, 0, jnp.where(kj > qi, 1, 2))
        gap = jnp.abs(qi - kj) * tq * jnp.ones((1, LANES), jnp.int32)
        return case, -c2 * gap.astype(F32)

    def scores(c, j, case):
        cs = slice(c * DIFF_HD, (c + 1) * DIFF_HD)
        return _dot_nt(q_ref[0, 0, :, cs], k_ref[0, 0, j * tk:(j + 1) * tk, cs]) + tile_ref[case]

    def lane_groups(x):
        return [x[:, g * LANES:(g + 1) * LANES] for g in range(groups)]

    @pl.when(sweep == 0)
    def _():
        m = [m_ref[c] for c in range(2)]
        for j in range(sub_blocks):
            case, cb = block_bias(j)
            for c in range(2):
                m[c] = jnp.maximum(m[c], functools.reduce(jnp.maximum, lane_groups(scores(c, j, case))) + cb)
        for c in range(2):
            m_ref[c] = m[c]

    @pl.when(sweep == 1)
    def _():
        m = [m_ref[c] for c in range(2)]
        l = [l_ref[c] for c in range(2)]
        pv = [None, None]
        for j in range(sub_blocks):
            case, cb = block_bias(j)
            v = v_ref[0, 0, j * tk:(j + 1) * tk, :]
            for c in range(2):
                shift = jnp.concatenate([cb - m[c]] * groups, axis=1)
                p = jnp.exp2(scores(c, j, case) + shift)
                l[c] = l[c] + functools.reduce(jnp.add, lane_groups(p))
                part = _dot(p.astype(BF16), v)
                pv[c] = part if pv[c] is None else pv[c] + part
        for c in range(2):
            l_ref[c] = l[c]
            acc_ref[c] += pv[c]

    @pl.when((sweep == 1) & (ki == pl.num_programs(4) - 1))
    def _():
        lp = lam_ref[...]
        lam = (jnp.exp(jnp.sum(lp[0:1] * lp[1:2], axis=-1, keepdims=True))
               - jnp.exp(jnp.sum(lp[2:3] * lp[3:4], axis=-1, keepdims=True)) + LAM_INIT)
        l0 = jnp.sum(l_ref[0], axis=-1, keepdims=True)
        l1 = jnp.sum(l_ref[1], axis=-1, keepdims=True)
        o = acc_ref[0] / l0 - lam * (acc_ref[1] / l1)
        o_ref[0] = (_rms(o, gain_ref[...]) * (1.0 - LAM_INIT)).astype(o_ref.dtype)


def diff_attention(proj_b, slopes, diff_lambda, gain, *, tile=512, kv_blocks=16):
    _, b, s, e = proj_b.shape
    tq = tk = _tile(s, tile)
    tkv = _tile(s, kv_blocks * tk)
    return pl.pallas_call(
        _attn_kernel,
        grid=(b, DIFF_HEADS, s // tq, 2, s // tkv),
        in_specs=[pl.BlockSpec(memory_space=pltpu.SMEM),
                  pl.BlockSpec((1, 1, tq, e), lambda b_, h, qi, sw, ki: (h, b_, qi, 0)),
                  pl.BlockSpec((1, 1, tkv, e), lambda b_, h, qi, sw, ki: (DIFF_HEADS + h, b_, ki, 0)),
                  pl.BlockSpec((1, 1, tkv, e), lambda b_, h, qi, sw, ki: (2 * DIFF_HEADS + h, b_, ki * sw, 0)),
                  pl.BlockSpec((4, DIFF_HD), lambda b_, h, qi, sw, ki: (0, 0)),
                  pl.BlockSpec((1, e), lambda b_, h, qi, sw, ki: (0, 0))],
        out_specs=pl.BlockSpec((1, tq, e), lambda b_, h, qi, sw, ki: (b_, qi, h)),
        out_shape=jax.ShapeDtypeStruct((b, s, DIFF_WIDTH), BF16),
        scratch_shapes=[pltpu.VMEM((2, tq, LANES), F32),
                        pltpu.VMEM((2, tq, LANES), F32),
                        pltpu.VMEM((2, tq, e), F32),
                        pltpu.VMEM((3, tq, tk), F32)],
        compiler_params=_params("parallel", "parallel", "parallel", "arbitrary", "arbitrary"),
        name="diff_attention",
    )(slopes, proj_b, proj_b, proj_b, diff_lambda, gain.reshape(1, e))


def _merge_kernel(o_ref, z_ref, ga_ref, gb_ref, ob_ref, wa_ref, wb_ref, gain_ref, out_ref, oa_ref):
    @pl.when(pl.program_id(1) == 0)
    def _():
        gain = gain_ref[...]
        for h in range(GDN_HEADS):
            hs = slice(h * GDN_DK, (h + 1) * GDN_DK)
            z = z_ref[:, hs]
            oa_ref[:, hs] = (_rms(o_ref[0, :, hs] + o_ref[1, :, hs], gain) * (z * _sigmoid(z))).astype(BF16)

    out_ref[...] = (_sigmoid(ga_ref[...]) * _dot(oa_ref[...], wa_ref[...])
                    + _sigmoid(gb_ref[...]) * _dot(ob_ref[...], wb_ref[...])).astype(out_ref.dtype)


def merge_branches(o_gdn, proj_a, o_b, w_a, w_b, gain, *, tm=256, tn=2048):
    t = o_b.shape[0]
    tm = _tile(t, tm)
    z_blk = GDN_QKV // GDN_WIDTH
    ga_blk = (GDN_QKV + GDN_WIDTH) // tn
    gb_blk = (GDN_QKV + GDN_WIDTH + D_MODEL) // tn
    return pl.pallas_call(
        _merge_kernel,
        grid=(t // tm, D_MODEL // tn),
        in_specs=[pl.BlockSpec((2, tm, GDN_WIDTH), lambda i, j: (0, i, 0)),
                  pl.BlockSpec((tm, GDN_WIDTH), lambda i, j: (i, z_blk)),
                  pl.BlockSpec((tm, tn), lambda i, j: (i, ga_blk + j)),
                  pl.BlockSpec((tm, tn), lambda i, j: (i, gb_blk + j)),
                  pl.BlockSpec((tm, DIFF_WIDTH), lambda i, j: (i, 0)),
                  pl.BlockSpec((GDN_WIDTH, tn), lambda i, j: (0, j)),
                  pl.BlockSpec((DIFF_WIDTH, tn), lambda i, j: (0, j)),
                  pl.BlockSpec((1, GDN_DK), lambda i, j: (0, 0))],
        out_specs=pl.BlockSpec((tm, tn), lambda i, j: (i, j)),
        out_shape=jax.ShapeDtypeStruct((t, D_MODEL), BF16),
        scratch_shapes=[pltpu.VMEM((tm, GDN_WIDTH), BF16)],
        compiler_params=_params("parallel", "arbitrary"),
        name="merge_branches",
    )(o_gdn, proj_a, proj_a, proj_a, o_b, w_a, w_b, gain.reshape(1, GDN_DK))


def _extract_distinct(x, n):
    vals, cnts = [], []
    for _ in range(n):
        m = jnp.max(x, axis=0, keepdims=True)
        eq = x == m
        cnt = jnp.sum(jnp.where(eq, 1.0, 0.0), axis=0, keepdims=True)
        vals.append(m)
        cnts.append(jnp.where(m == NEG_INF, 0.0, cnt))
        x = jnp.where(eq, NEG_INF, x)
    return vals, cnts


def _pair_sums(at, bt):
    s8 = SUBLANES
    return jnp.concatenate([at[k:k + 1] + bt[0:s8] for k in range(s8)]
                           + [at[0:1] + bt[s8:], at[s8:] + bt[0:1]], axis=0)


def _kth_largest(x, mult, k):
    tau = jnp.zeros_like(x[0:1])
    cum = jnp.zeros_like(x[0:1])
    for _ in range(k):
        m = jnp.max(x, axis=0, keepdims=True)
        eq = x == m
        tau = jnp.where(cum < k, m, tau)
        cum = cum + jnp.sum(jnp.where(eq, mult, 0.0), axis=0, keepdims=True)
        x = jnp.where(eq, NEG_INF, x)
    return tau


def _route_kernel(q_ref, keys_ref, a_ref, b_ref, rows_ref):
    half = PEER_DQ // 2
    s1 = _dot_nt(keys_ref[0, 0], q_ref[:, 0:half])
    s2 = _dot_nt(keys_ref[0, 1], q_ref[:, half:PEER_DQ])
    v1, n1 = _extract_distinct(s1, PEER_TOPK)
    v2, n2 = _extract_distinct(s2, PEER_TOPK)
    at = jnp.concatenate([(v - v1[0]) * LOG2E for v in v1], axis=0)
    bt = jnp.concatenate([(v - v2[0]) * LOG2E for v in v2], axis=0)
    s8 = SUBLANES
    na = jnp.concatenate(n1, axis=0)
    nb = jnp.concatenate(n2, axis=0)
    mult = jnp.concatenate([na[k:k + 1] * nb[0:s8] for k in range(s8)]
                           + [na[0:1] * nb[s8:], na[s8:] * nb[0:1]], axis=0)
    cand = _pair_sums(at, bt)
    tau = _kth_largest(cand, mult, PEER_TOPK)
    z = jnp.sum(jnp.where(cand >= tau, mult * jnp.exp2(cand), 0.0), axis=0, keepdims=True)
    nlz = -jnp.log2(z)
    a_ref[0] = (s1 - v1[0]) * LOG2E
    b_ref[0] = (s2 - v2[0]) * LOG2E + nlz
    tau_z = jnp.min(jnp.where(cand >= tau, _pair_sums(at, bt + nlz), jnp.inf), axis=0, keepdims=True)
    rows_ref[0] = jnp.concatenate([tau_z, jnp.zeros((s8 - 1, tau_z.shape[1]), F32)], axis=0)


def peer_route(q, keys, *, tt=512):
    t = q.shape[0]
    tt = _tile(t, tt)
    tab = jax.ShapeDtypeStruct((PEER_HEADS, PEER_NKEYS, t), F32)
    tab_spec = pl.BlockSpec((1, PEER_NKEYS, tt), lambda i, p: (p, 0, i))
    return pl.pallas_call(
        _route_kernel,
        grid=(t // tt, PEER_HEADS),
        in_specs=[pl.BlockSpec((tt, PEER_DQ), lambda i, p: (i, p)),
                  pl.BlockSpec((1, 2, PEER_NKEYS, PEER_DQ // 2), lambda i, p: (p, 0, 0, 0))],
        out_specs=[tab_spec, tab_spec, pl.BlockSpec((1, SUBLANES, tt), lambda i, p: (p, 0, i))],
        out_shape=[tab, tab, jax.ShapeDtypeStruct((PEER_HEADS, SUBLANES, t), F32)],
        compiler_params=_params("parallel", "parallel"),
        name="peer_route",
    )(q, keys)


def _gelu_tanh(x):
    return x * (0.5 * (1.0 + jnp.tanh(math.sqrt(2.0 / math.pi) * (x + 0.044715 * (x * x * x)))))


def _peer_kernel(x_ref, gf_ref, u_ref, v_ref, a_ref, b_ref, rows_ref, gfin_ref,
                 o_ref, h_ref, gate_ref, coef_ref):
    e = pl.program_id(1)
    te, tt = u_ref.shape[0], x_ref.shape[0]
    rows = te // PEER_NKEYS

    @pl.when(e == 0)
    def _():
        h_ref[...] = jnp.transpose(_rms(x_ref[...], gf_ref[...])).astype(BF16)
        o_ref[...] = jnp.zeros_like(o_ref)

    for il in range(rows):
        rs = slice(il * PEER_NKEYS, (il + 1) * PEER_NKEYS)
        a_rows = [a_ref[p, pl.ds(e * rows + il, 1), :] for p in range(PEER_HEADS)]
        for lb in range(tt // LANES):
            ls = slice(lb * LANES, (lb + 1) * LANES)
            g = None
            for p in range(PEER_HEADS):
                pair = a_rows[p][:, ls] + b_ref[p, :, ls]
                w = jnp.where(pair >= rows_ref[p, 0:1, ls], jnp.exp2(pair), 0.0)
                g = w if g is None else g + w
            gate_ref[rs, ls] = g
    act = _gelu_tanh(_dot(u_ref[...], h_ref[...]))
    coef_ref[...] = (act * gate_ref[...]).astype(BF16)
    o_ref[...] += _dot_tn(coef_ref[...], v_ref[...])

    @pl.when(e == pl.num_programs(1) - 1)
    def _():
        o_ref[...] = _rms(x_ref[...] + o_ref[...], gfin_ref[...])


def peer_experts(x1, gain_ffn, u, v, a, b, rows, gain_final, *, tt=1024, te=512):
    t, d = x1.shape
    tt, te = _tile(t, tt), _tile(PEER_EXPERTS, te)
    once = pl.Buffered(1)
    tab_spec = pl.BlockSpec((PEER_HEADS, PEER_NKEYS, tt), lambda i, e: (0, 0, i))
    return pl.pallas_call(
        _peer_kernel,
        grid=(t // tt, PEER_EXPERTS // te),
        in_specs=[pl.BlockSpec((tt, d), lambda i, e: (i, 0), pipeline_mode=once),
                  pl.BlockSpec((1, d), lambda i, e: (0, 0)),
                  pl.BlockSpec((te, d), lambda i, e: (e, 0)),
                  pl.BlockSpec((te, d), lambda i, e: (e, 0)),
                  tab_spec, tab_spec,
                  pl.BlockSpec((PEER_HEADS, SUBLANES, tt), lambda i, e: (0, 0, i)),
                  pl.BlockSpec((1, d), lambda i, e: (0, 0))],
        out_specs=pl.BlockSpec((tt, d), lambda i, e: (i, 0), pipeline_mode=once),
        out_shape=jax.ShapeDtypeStruct((t, d), F32),
        scratch_shapes=[pltpu.VMEM((d, tt), BF16),
                        pltpu.VMEM((te, tt), F32),
                        pltpu.VMEM((te, tt), BF16)],
        compiler_params=_params("parallel", "arbitrary", vmem_limit=PEER_VMEM_LIMIT_BYTES),
        name="peer_experts",
    )(x1, gain_ffn.reshape(1, d), u, v, a, b, rows, gain_final.reshape(1, d))


def _prepare_weights(w_in, gdn_a_log, gdn_dt_bias, w_branch_a, w_branch_b, w_out, peer_w_q,
                     peer_sub_keys, peer_u, peer_v):
    o_z = GDN_QKV + GDN_WIDTH
    o_ab = o_z + 4 * GDN_HEADS
    o_b = o_ab + 2 * DIFF_QK + DIFF_WIDTH
    w = w_in[0]
    pad = jnp.zeros((D_MODEL, LANES - 4 * GDN_HEADS), F32)
    row = lambda p: jnp.concatenate([p[0].reshape(1, -1), jnp.zeros((1, LANES - 2 * GDN_HEADS), F32)], axis=1)
    return dict(
        w_a=jnp.concatenate([w[:, :o_z], w[:, o_b:]], axis=1).astype(BF16),
        w_ab=jnp.concatenate([w[:, o_z:o_ab], pad], axis=1).astype(BF16),
        w_b=w[:, o_ab:o_b].astype(BF16),
        alog=row(gdn_a_log), dtb=row(gdn_dt_bias),
        qscale=jnp.concatenate([jnp.full((1, DIFF_QK), DIFF_HD ** -0.5 * LOG2E, F32),
                                jnp.ones((1, DIFF_QK + DIFF_WIDTH), F32)], axis=1),
        w_branch_a=w_branch_a[0].astype(BF16), w_branch_b=w_branch_b[0].astype(BF16),
        w_out=w_out[0].astype(BF16), w_q=peer_w_q[0].astype(BF16),
        keys=peer_sub_keys[0].astype(BF16), u=peer_u[0].astype(BF16), v=peer_v[0].astype(BF16),
    )


def _trunk(x, pw, norm_mix_gain, conv_w, gdn_norm_gain, diff_lambda, diff_norm_gain,
           norm_ffn_gain, norm_final_gain, slopes):
    b, s, d = x.shape
    t = b * s
    xt = x.reshape(t, d)
    g_mix = norm_mix_gain[0]

    proj_a = norm_matmul(xt, g_mix, pw["w_a"], F32, name="in_proj_a")
    proj_b = norm_matmul(xt, g_mix, pw["w_b"], BF16, epilogue=_scale_epilogue,
                         extra=(pw["qscale"],), group=2 * DIFF_HD, name="in_proj_b")
    gb = norm_matmul(xt, g_mix, pw["w_ab"], F32, epilogue=_gate_epilogue,
                     extra=(pw["alog"], pw["dtb"]), name="in_proj_gates")

    qkv_n = conv_prep(proj_a.reshape(b, s, -1), conv_w[0])
    g = gb[:, :2 * GDN_HEADS].reshape(b, s, 2, GDN_HEADS)
    beta = gb[:, 2 * GDN_HEADS:4 * GDN_HEADS].reshape(b, s, 2, GDN_HEADS)
    gb_col = jnp.transpose(jnp.concatenate([g, beta], axis=-1), (0, 2, 1, 3))
    gb_col = jnp.pad(gb_col, ((0, 0), (0, 0), (0, 0), (0, LANES - 2 * GDN_HEADS)))
    g_row = jnp.transpose(g, (0, 2, 3, 1))
    o_gdn = gdn_scan(qkv_n, gb_col, g_row).reshape(2, t, GDN_WIDTH)

    o_b = diff_attention(proj_b.reshape(-1, b, s, 2 * DIFF_HD), slopes, diff_lambda[0], diff_norm_gain[0])
    o_b = o_b.reshape(t, DIFF_WIDTH)

    merged = merge_branches(o_gdn, proj_a, o_b, pw["w_branch_a"], pw["w_branch_b"], gdn_norm_gain[0])
    x1 = matmul_residual(merged, pw["w_out"], xt, name="out_proj")

    q = norm_matmul(x1, norm_ffn_gain[0], pw["w_q"], BF16, name="peer_query")
    a, bb, rows = peer_route(q, pw["keys"])
    y = peer_experts(x1, norm_ffn_gain[0], pw["u"], pw["v"], a, bb, rows, norm_final_gain)
    return y.reshape(b, s, d)


def kernel(x_prompt, x_sample, norm_mix_gain, w_in, conv_w, gdn_a_log, gdn_dt_bias, gdn_norm_gain,
           diff_lambda, diff_norm_gain, w_branch_a, w_branch_b, w_out, norm_ffn_gain,
           peer_w_q, peer_sub_keys, peer_u, peer_v, norm_final_gain):
    pw = _prepare_weights(w_in, gdn_a_log, gdn_dt_bias, w_branch_a, w_branch_b, w_out, peer_w_q,
                          peer_sub_keys, peer_u, peer_v)
    slopes = 2.0 ** (-8.0 * jnp.arange(1, DIFF_HEADS + 1, dtype=F32) / DIFF_HEADS)
    run = functools.partial(_trunk, pw=pw, norm_mix_gain=norm_mix_gain, conv_w=conv_w,
                            gdn_norm_gain=gdn_norm_gain, diff_lambda=diff_lambda,
                            diff_norm_gain=diff_norm_gain, norm_ffn_gain=norm_ffn_gain,
                            norm_final_gain=norm_final_gain, slopes=slopes)
    return (run(x_prompt), run(x_sample))
```

```python
import functools
import math

import jax
import jax.numpy as jnp
from jax import lax
from jax.experimental import pallas as pl
from jax.experimental.pallas import tpu as pltpu

F32 = jnp.float32
BF16 = jnp.bfloat16

D_MODEL = 2048
GDN_HEADS = 8
GDN_DK = 128
GDN_QK = 1024
GDN_WIDTH = 1024
GDN_QKV = 3072
GDN_CONV = 5
GDN_CHUNK = 128
DIFF_HEADS = 4
DIFF_HD = 128
DIFF_QK = 1024
DIFF_WIDTH = 1024
PEER_HEADS = 8
PEER_NKEYS = 128
PEER_EXPERTS = PEER_NKEYS * PEER_NKEYS
PEER_DQ = 256
PEER_TOPK = 16
NORM_EPS = 1e-6
LAM_INIT = 0.8 - 0.6 * math.exp(-0.3 * 0)

LANES = 128
SUBLANES = 8
VMEM_LIMIT_BYTES = 56 * 1024 * 1024
PEER_VMEM_LIMIT_BYTES = 60 * 1024 * 1024

NEG_INF = float("-inf")
LOG2E = math.log2(math.e)


def _params(*sem, vmem_limit=VMEM_LIMIT_BYTES):
    return pltpu.CompilerParams(dimension_semantics=sem, vmem_limit_bytes=vmem_limit)


def _tile(n, pref):
    t = min(n, pref)
    while n % t:
        t //= 2
    return t


def _dot(a, b):
    return jnp.dot(a, b, preferred_element_type=F32)


def _dot_nt(a, b):
    return lax.dot_general(a, b, (((1,), (1,)), ((), ())), preferred_element_type=F32)


def _dot_tn(a, b):
    return lax.dot_general(a, b, (((0,), (0,)), ((), ())), preferred_element_type=F32)


def _sigmoid(x):
    return 1.0 / (1.0 + jnp.exp(-x))


def _rms(x, gain):
    return x * lax.rsqrt(jnp.mean(x * x, axis=-1, keepdims=True) + NORM_EPS) * gain


def _norm_matmul_kernel(x_ref, g_ref, w_ref, *rest, epilogue, n_extra):
    extra, o_ref, h_ref = rest[:n_extra], rest[n_extra], rest[n_extra + 1]

    @pl.when(pl.program_id(1) == 0)
    def _():
        h_ref[...] = _rms(x_ref[...], g_ref[...]).astype(BF16)

    acc = _dot(h_ref[...], w_ref[...])
    if epilogue is not None:
        acc = epilogue(acc, *[e[...] for e in extra])
    if len(o_ref.shape) == 2:
        o_ref[...] = acc.astype(o_ref.dtype)
    else:
        gw = o_ref.shape[2]
        for gi in range(o_ref.shape[0]):
            o_ref[gi] = acc[:, gi * gw:(gi + 1) * gw].astype(o_ref.dtype)


def norm_matmul(x, gain, w, out_dtype, *, epilogue=None, extra=(), tm=1024, tn=1024, group=None, name):
    t, d = x.shape
    n = w.shape[1]
    tm, tn = _tile(t, tm), _tile(n, tn)
    kern = functools.partial(_norm_matmul_kernel, epilogue=epilogue, n_extra=len(extra))
    if group is None:
        out_spec = pl.BlockSpec((tm, tn), lambda i, j: (i, j))
        out_shape = jax.ShapeDtypeStruct((t, n), out_dtype)
    else:
        out_spec = pl.BlockSpec((tn // group, tm, group), lambda i, j: (j, i, 0))
        out_shape = jax.ShapeDtypeStruct((n // group, t, group), out_dtype)
    return pl.pallas_call(
        kern,
        grid=(t // tm, n // tn),
        in_specs=[pl.BlockSpec((tm, d), lambda i, j: (i, 0)),
                  pl.BlockSpec((1, d), lambda i, j: (0, 0)),
                  pl.BlockSpec((d, tn), lambda i, j: (0, j))]
                 + [pl.BlockSpec((1, tn), lambda i, j: (0, j)) for _ in extra],
        out_specs=out_spec,
        out_shape=out_shape,
        scratch_shapes=[pltpu.VMEM((tm, d), BF16)],
        compiler_params=_params("parallel", "arbitrary"),
        name=name,
    )(x, gain.reshape(1, d), w, *extra)


def _gate_epilogue(acc, alog, dtb):
    z = acc + dtb
    softplus = jnp.maximum(z, 0.0) + jnp.log(1.0 + jnp.exp(-jnp.abs(z)))
    g = -jnp.exp(alog) * softplus
    lane = lax.broadcasted_iota(jnp.int32, acc.shape, 1)
    return jnp.where(lane < 2 * GDN_HEADS, g, _sigmoid(acc))


def _scale_epilogue(acc, scale):
    return acc * scale


def _matmul_res_kernel(a_ref, w_ref, r_ref, o_ref):
    o_ref[...] = r_ref[...] + _dot(a_ref[...], w_ref[...])


def matmul_residual(a, w, res, *, tm=1024, tn=1024, name):
    t, k = a.shape
    n = w.shape[1]
    tm, tn = _tile(t, tm), _tile(n, tn)
    return pl.pallas_call(
        _matmul_res_kernel,
        grid=(t // tm, n // tn),
        in_specs=[pl.BlockSpec((tm, k), lambda i, j: (i, 0)),
                  pl.BlockSpec((k, tn), lambda i, j: (0, j)),
                  pl.BlockSpec((tm, tn), lambda i, j: (i, j))],
        out_specs=pl.BlockSpec((tm, tn), lambda i, j: (i, j)),
        out_shape=jax.ShapeDtypeStruct((t, n), F32),
        compiler_params=_params("parallel", "arbitrary"),
        name=name,
    )(a, w, res)


def _conv_kernel(cur_ref, prev_ref, next_ref, w_ref, o_ref, pad_ref):
    i, c = pl.program_id(1), pl.program_id(2)
    ts = cur_ref.shape[1]
    halo = SUBLANES
    pad_ref[0:halo, :] = jnp.where(i > 0, prev_ref[0], 0.0)
    pad_ref[halo:halo + ts, :] = cur_ref[0]
    pad_ref[halo + ts:2 * halo + ts, :] = jnp.where(i < pl.num_programs(1) - 1, next_ref[0], 0.0)
    w = w_ref[...]
    first = halo - (GDN_CONV - 1) // 2
    acc = w[0:1, :] * pad_ref[first:first + ts, :]
    for k in range(1, GDN_CONV):
        acc = acc + w[k:k + 1, :] * pad_ref[first + k:first + k + ts, :]
    y = acc * _sigmoid(acc)
    heads_per_block = y.shape[1] // GDN_DK
    for hh in range(heads_per_block):
        hs = slice(hh * GDN_DK, (hh + 1) * GDN_DK)
        head = c * heads_per_block + hh
        yh = y[:, hs]
        rs = lax.rsqrt(jnp.sum(yh * yh, axis=-1, keepdims=True) + 1e-6)
        f = jnp.where(head < GDN_HEADS, rs * (GDN_DK ** -0.5), jnp.where(head < 2 * GDN_HEADS, rs, 1.0))
        o_ref[0, :, hs] = (yh * f).astype(o_ref.dtype)


def conv_prep(proj_a, conv_w, *, ts=2048, width=2 * LANES):
    b, s, _ = proj_a.shape
    ts = _tile(s, ts)
    r = ts // SUBLANES
    last = s // SUBLANES - 1
    return pl.pallas_call(
        _conv_kernel,
        grid=(b, s // ts, GDN_QKV // width),
        in_specs=[pl.BlockSpec((1, ts, width), lambda b_, i, c: (b_, i, c)),
                  pl.BlockSpec((1, SUBLANES, width), lambda b_, i, c: (b_, jnp.maximum(i * r - 1, 0), c)),
                  pl.BlockSpec((1, SUBLANES, width), lambda b_, i, c: (b_, jnp.minimum((i + 1) * r, last), c)),
                  pl.BlockSpec((GDN_CONV, width), lambda b_, i, c: (0, c))],
        out_specs=pl.BlockSpec((1, ts, width), lambda b_, i, c: (b_, i, c)),
        out_shape=jax.ShapeDtypeStruct((b, s, GDN_QKV), BF16),
        scratch_shapes=[pltpu.VMEM((ts + 2 * SUBLANES, width), F32)],
        compiler_params=_params("parallel", "parallel", "parallel"),
        name="gdn_conv_prep",
    )(proj_a, proj_a, proj_a, conv_w)


def _split3(x):
    hi = x.astype(BF16)
    r = x - hi.astype(F32)
    mid = r.astype(BF16)
    lo = (r - mid.astype(F32)).astype(BF16)
    return hi, mid, lo


def _gdn_kernel(q_ref, k_ref, v_ref, gb_ref, gr_ref, o_ref, state_ref):
    d, i = pl.program_id(1), pl.program_id(2)
    c = GDN_CHUNK

    @pl.when(i == 0)
    def _():
        state_ref[...] = jnp.zeros_like(state_ref)

    row = lax.broadcasted_iota(jnp.int32, (c, c), 0)
    col = lax.broadcasted_iota(jnp.int32, (c, c), 1)
    order = jnp.where(d == 0, row - col, col - row)
    incl = order >= 0
    strict = order > 0
    eye = jnp.where(row == col, 1.0, 0.0)
    ones_incl = jnp.where(incl, 1.0, 0.0).astype(BF16)

    gb = gb_ref[0, 0]
    g_col = gb[:, 0:GDN_HEADS]
    g_row = gr_ref[0, 0]
    gcum_col = sum(_dot(ones_incl, p) for p in _split3(g_col))
    gcum_row = sum(_dot_nt(p, ones_incl) for p in _split3(g_row))
    g_tot = jnp.sum(g_col, axis=0, keepdims=True)
    beta_all = gb[:, GDN_HEADS:2 * GDN_HEADS]

    heads = range(GDN_HEADS)
    hs = [slice(h * GDN_DK, (h + 1) * GDN_DK) for h in heads]
    q = [q_ref[0, :, hs[h]] for h in heads]
    k = [k_ref[0, :, hs[h]] for h in heads]
    beta = [beta_all[:, h:h + 1] for h in heads]
    gcc = [gcum_col[:, h:h + 1] for h in heads]
    gt = [g_tot[:, h:h + 1] for h in heads]
    kf = [k[h].astype(F32) for h in heads]
    kb = [kf[h] * beta[h] for h in heads]
    akk = [_dot_nt(kb[h].astype(BF16), k[h]) for h in heads]
    qk = [_dot_nt(q[h], k[h]) for h in heads]
    decay = [jnp.where(incl, jnp.exp(jnp.where(incl, gcc[h] - gcum_row[h:h + 1, :], 0.0)), 0.0)
             for h in heads]

    p = [-jnp.where(strict, akk[h] * decay[h], 0.0) for h in heads]
    t_inv = [eye + p[h] for h in heads]
    for _ in range(int(math.log2(c)) - 1):
        pb = [p[h].astype(BF16) for h in heads]
        p = [_dot(pb[h], pb[h]) for h in heads]
        t_inv = [t_inv[h] + _dot(t_inv[h].astype(BF16), p[h].astype(BF16)) for h in heads]

    eg = [jnp.exp(gcc[h]) for h in heads]
    rhs = [jnp.concatenate([(v_ref[0, :, hs[h]].astype(F32) * beta[h]).astype(BF16),
                            (kb[h] * eg[h]).astype(BF16)], axis=1) for h in heads]
    uw = [_dot(t_inv[h].astype(BF16), rhs[h]) for h in heads]
    lhs_o = [jnp.concatenate([(q[h].astype(F32) * eg[h]).astype(BF16),
                              jnp.where(incl, qk[h] * decay[h], 0.0).astype(BF16)], axis=1) for h in heads]
    k_tail = [(kf[h] * jnp.exp(gt[h] - gcc[h])).astype(BF16) for h in heads]
    s = [state_ref[h] for h in heads]
    sb = [s[h].astype(BF16) for h in heads]
    vb = [(uw[h][:, :GDN_DK] - _dot(uw[h][:, GDN_DK:].astype(BF16), sb[h])).astype(BF16) for h in heads]
    o = [_dot(lhs_o[h], jnp.concatenate([sb[h], vb[h]], axis=0)) for h in heads]
    s_new = [s[h] * jnp.exp(gt[h]) + _dot_tn(k_tail[h], vb[h]) for h in heads]
    for h in heads:
        state_ref[h] = s_new[h]
        o_ref[0, 0, :, hs[h]] = o[h]


def gdn_scan(qkv_n, gb_col, g_row):
    b, s, _ = qkv_n.shape
    c = GDN_CHUNK
    nc = s // c

    def cidx(d, i):
        return i + d * (nc - 1 - 2 * i)

    def qkv_spec(part):
        return pl.BlockSpec((1, c, GDN_QK), lambda b_, d, i: (b_, cidx(d, i), part))

    col_spec = pl.BlockSpec((1, 1, c, LANES), lambda b_, d, i: (b_, d, cidx(d, i), 0))
    row_spec = pl.BlockSpec((1, 1, GDN_HEADS, c), lambda b_, d, i: (b_, d, 0, cidx(d, i)))
    return pl.pallas_call(
        _gdn_kernel,
        grid=(b, 2, nc),
        in_specs=[qkv_spec(0), qkv_spec(1), qkv_spec(2), col_spec, row_spec],
        out_specs=pl.BlockSpec((1, 1, c, GDN_WIDTH), lambda b_, d, i: (d, b_, cidx(d, i), 0)),
        out_shape=jax.ShapeDtypeStruct((2, b, s, GDN_WIDTH), F32),
        scratch_shapes=[pltpu.VMEM((GDN_HEADS, GDN_DK, GDN_DK), F32)],
        compiler_params=_params("parallel", "parallel", "arbitrary"),
        name="gdn_scan",
    )(qkv_n, qkv_n, qkv_n, gb_col, g_row)


def _attn_kernel(slopes_ref, q_ref, k_ref, v_ref, lam_ref, gain_ref, o_ref, m_ref, l_ref, acc_ref, tile_ref):
    h, qi, sweep, ki = pl.program_id(1), pl.program_id(2), pl.program_id(3), pl.program_id(4)
    tq = tk = q_ref.shape[2]
    sub_blocks = k_ref.shape[2] // tk
    groups = tk // LANES
    c2 = slopes_ref[h] * LOG2E * jnp.ones((1, LANES), F32)

    @pl.when((sweep == 0) & (ki == 0))
    def _():
        m_ref[...] = jnp.full_like(m_ref, NEG_INF)
        rel = (lax.broadcasted_iota(jnp.int32, (tq, tk), 1)
               - lax.broadcasted_iota(jnp.int32, (tq, tk), 0)).astype(F32)
        t0 = c2[:, 0:1] * rel
        tile_ref[0] = t0
        tile_ref[1] = -t0
        tile_ref[2] = -jnp.abs(t0)

    @pl.when((sweep == 1) & (ki == 0))
    def _():
        l_ref[...] = jnp.zeros_like(l_ref)
        acc_ref[...] = jnp.zeros_like(acc_ref)
        for c in range(2):
            m_ref[c] = jnp.broadcast_to(jnp.max(m_ref[c], axis=-1, keepdims=True), (tq, LANES))

    def block_bias(j):
        kj = ki * sub_blocks + j
        case = jnp.where(kj < qi, 0, jnp.where(kj > qi, 1, 2))
        gap = jnp.abs(qi - kj) * tq * jnp.ones((1, LANES), jnp.int32)
        return case, -c2 * gap.astype(F32)

    def scores(c, j, case):
        cs = slice(c * DIFF_HD, (c + 1) * DIFF_HD)
        return _dot_nt(q_ref[0, 0, :, cs], k_ref[0, 0, j * tk:(j + 1) * tk, cs]) + tile_ref[case]

    def lane_groups(x):
        return [x[:, g * LANES:(g + 1) * LANES] for g in range(groups)]

    @pl.when(sweep == 0)
    def _():
        m = [m_ref[c] for c in range(2)]
        for j in range(sub_blocks):
            case, cb = block_bias(j)
            for c in range(2):
                m[c] = jnp.maximum(m[c], functools.reduce(jnp.maximum, lane_groups(scores(c, j, case))) + cb)
        for c in range(2):
            m_ref[c] = m[c]

    @pl.when(sweep == 1)
    def _():
        m = [m_ref[c] for c in range(2)]
        l = [l_ref[c] for c in range(2)]
        pv = [None, None]
        for j in range(sub_blocks):
            case, cb = block_bias(j)
            v = v_ref[0, 0, j * tk:(j + 1) * tk, :]
            for c in range(2):
                shift = jnp.concatenate([cb - m[c]] * groups, axis=1)
                p = jnp.exp2(scores(c, j, case) + shift)
                l[c] = l[c] + functools.reduce(jnp.add, lane_groups(p))
                part = _dot(p.astype(BF16), v)
                pv[c] = part if pv[c] is None else pv[c] + part
        for c in range(2):
            l_ref[c] = l[c]
            acc_ref[c] += pv[c]

    @pl.when((sweep == 1) & (ki == pl.num_programs(4) - 1))
    def _():
        lp = lam_ref[...]
        lam = (jnp.exp(jnp.sum(lp[0:1] * lp[1:2], axis=-1, keepdims=True))
               - jnp.exp(jnp.sum(lp[2:3] * lp[3:4], axis=-1, keepdims=True)) + LAM_INIT)
        l0 = jnp.sum(l_ref[0], axis=-1, keepdims=True)
        l1 = jnp.sum(l_ref[1], axis=-1, keepdims=True)
        o = acc_ref[0] / l0 - lam * (acc_ref[1] / l1)
        o_ref[0] = (_rms(o, gain_ref[...]) * (1.0 - LAM_INIT)).astype(o_ref.dtype)


def diff_attention(proj_b, slopes, diff_lambda, gain, *, tile=512, kv_blocks=16):
    _, b, s, e = proj_b.shape
    tq = tk = _tile(s, tile)
    tkv = _tile(s, kv_blocks * tk)
    return pl.pallas_call(
        _attn_kernel,
        grid=(b, DIFF_HEADS, s // tq, 2, s // tkv),
        in_specs=[pl.BlockSpec(memory_space=pltpu.SMEM),
                  pl.BlockSpec((1, 1, tq, e), lambda b_, h, qi, sw, ki: (h, b_, qi, 0)),
                  pl.BlockSpec((1, 1, tkv, e), lambda b_, h, qi, sw, ki: (DIFF_HEADS + h, b_, ki, 0)),
                  pl.BlockSpec((1, 1, tkv, e), lambda b_, h, qi, sw, ki: (2 * DIFF_HEADS + h, b_, ki * sw, 0)),
                  pl.BlockSpec((4, DIFF_HD), lambda b_, h, qi, sw, ki: (0, 0)),
                  pl.BlockSpec((1, e), lambda b_, h, qi, sw, ki: (0, 0))],
        out_specs=pl.BlockSpec((1, tq, e), lambda b_, h, qi, sw, ki: (b_, qi, h)),
        out_shape=jax.ShapeDtypeStruct((b, s, DIFF_WIDTH), BF16),
        scratch_shapes=[pltpu.VMEM((2, tq, LANES), F32),
                        pltpu.VMEM((2, tq, LANES), F32),
                        pltpu.VMEM((2, tq, e), F32),
                        pltpu.VMEM((3, tq, tk), F32)],
        compiler_params=_params("parallel", "parallel", "parallel", "arbitrary", "arbitrary"),
        name="diff_attention",
    )(slopes, proj_b, proj_b, proj_b, diff_lambda, gain.reshape(1, e))


def _merge_kernel(o_ref, z_ref, ga_ref, gb_ref, ob_ref, wa_ref, wb_ref, gain_ref, out_ref, oa_ref):
    @pl.when(pl.program_id(1) == 0)
    def _():
        gain = gain_ref[...]
        for h in range(GDN_HEADS):
            hs = slice(h * GDN_DK, (h + 1) * GDN_DK)
            z = z_ref[:, hs]
            oa_ref[:, hs] = (_rms(o_ref[0, :, hs] + o_ref[1, :, hs], gain) * (z * _sigmoid(z))).astype(BF16)

    out_ref[...] = (_sigmoid(ga_ref[...]) * _dot(oa_ref[...], wa_ref[...])
                    + _sigmoid(gb_ref[...]) * _dot(ob_ref[...], wb_ref[...])).astype(out_ref.dtype)


def merge_branches(o_gdn, proj_a, o_b, w_a, w_b, gain, *, tm=256, tn=2048):
    t = o_b.shape[0]
    tm = _tile(t, tm)
    z_blk = GDN_QKV // GDN_WIDTH
    ga_blk = (GDN_QKV + GDN_WIDTH) // tn
    gb_blk = (GDN_QKV + GDN_WIDTH + D_MODEL) // tn
    return pl.pallas_call(
        _merge_kernel,
        grid=(t // tm, D_MODEL // tn),
        in_specs=[pl.BlockSpec((2, tm, GDN_WIDTH), lambda i, j: (0, i, 0)),
                  pl.BlockSpec((tm, GDN_WIDTH), lambda i, j: (i, z_blk)),
                  pl.BlockSpec((tm, tn), lambda i, j: (i, ga_blk + j)),
                  pl.BlockSpec((tm, tn), lambda i, j: (i, gb_blk + j)),
                  pl.BlockSpec((tm, DIFF_WIDTH), lambda i, j: (i, 0)),
                  pl.BlockSpec((GDN_WIDTH, tn), lambda i, j: (0, j)),
                  pl.BlockSpec((DIFF_WIDTH, tn), lambda i, j: (0, j)),
                  pl.BlockSpec((1, GDN_DK), lambda i, j: (0, 0))],
        out_specs=pl.BlockSpec((tm, tn), lambda i, j: (i, j)),
        out_shape=jax.ShapeDtypeStruct((t, D_MODEL), BF16),
        scratch_shapes=[pltpu.VMEM((tm, GDN_WIDTH), BF16)],
        compiler_params=_params("parallel", "arbitrary"),
        name="merge_branches",
    )(o_gdn, proj_a, proj_a, proj_a, o_b, w_a, w_b, gain.reshape(1, GDN_DK))


def _extract_distinct(x, n):
    vals, cnts = [], []
    for _ in range(n):
        m = jnp.max(x, axis=0, keepdims=True)
        eq = x == m
        cnt = jnp.sum(jnp.where(eq, 1.0, 0.0), axis=0, keepdims=True)
        vals.append(m)
        cnts.append(jnp.where(m == NEG_INF, 0.0, cnt))
        x = jnp.where(eq, NEG_INF, x)
    return vals, cnts


def _pair_sums(at, bt):
    s8 = SUBLANES
    return jnp.concatenate([at[k:k + 1] + bt[0:s8] for k in range(s8)]
                           + [at[0:1] + bt[s8:], at[s8:] + bt[0:1]], axis=0)


def _kth_largest(x, mult, k):
    tau = jnp.zeros_like(x[0:1])
    cum = jnp.zeros_like(x[0:1])
    for _ in range(k):
        m = jnp.max(x, axis=0, keepdims=True)
        eq = x == m
        tau = jnp.where(cum < k, m, tau)
        cum = cum + jnp.sum(jnp.where(eq, mult, 0.0), axis=0, keepdims=True)
        x = jnp.where(eq, NEG_INF, x)
    return tau


def _route_kernel(q_ref, keys_ref, a_ref, b_ref, rows_ref):
    half = PEER_DQ // 2
    s1 = _dot_nt(keys_ref[0, 0], q_ref[:, 0:half])
    s2 = _dot_nt(keys_ref[0, 1], q_ref[:, half:PEER_DQ])
    v1, n1 = _extract_distinct(s1, PEER_TOPK)
    v2, n2 = _extract_distinct(s2, PEER_TOPK)
    at = jnp.concatenate([(v - v1[0]) * LOG2E for v in v1], axis=0)
    bt = jnp.concatenate([(v - v2[0]) * LOG2E for v in v2], axis=0)
    s8 = SUBLANES
    na = jnp.concatenate(n1, axis=0)
    nb = jnp.concatenate(n2, axis=0)
    mult = jnp.concatenate([na[k:k + 1] * nb[0:s8] for k in range(s8)]
                           + [na[0:1] * nb[s8:], na[s8:] * nb[0:1]], axis=0)
    cand = _pair_sums(at, bt)
    tau = _kth_largest(cand, mult, PEER_TOPK)
    z = jnp.sum(jnp.where(cand >= tau, mult * jnp.exp2(cand), 0.0), axis=0, keepdims=True)
    nlz = -jnp.log2(z)
    a_ref[0] = (s1 - v1[0]) * LOG2E
    b_ref[0] = (s2 - v2[0]) * LOG2E + nlz
    tau_z = jnp.min(jnp.where(cand >= tau, _pair_sums(at, bt + nlz), jnp.inf), axis=0, keepdims=True)
    rows_ref[0] = jnp.concatenate([tau_z, jnp.zeros((s8 - 1, tau_z.shape[1]), F32)], axis=0)


def peer_route(q, keys, *, tt=1024):
    t = q.shape[0]
    tt = _tile(t, tt)
    tab = jax.ShapeDtypeStruct((PEER_HEADS, PEER_NKEYS, t), F32)
    tab_spec = pl.BlockSpec((1, PEER_NKEYS, tt), lambda i, p: (p, 0, i))
    return pl.pallas_call(
        _route_kernel,
        grid=(t // tt, PEER_HEADS),
        in_specs=[pl.BlockSpec((tt, PEER_DQ), lambda i, p: (i, p)),
                  pl.BlockSpec((1, 2, PEER_NKEYS, PEER_DQ // 2), lambda i, p: (p, 0, 0, 0))],
        out_specs=[tab_spec, tab_spec, pl.BlockSpec((1, SUBLANES, tt), lambda i, p: (p, 0, i))],
        out_shape=[tab, tab, jax.ShapeDtypeStruct((PEER_HEADS, SUBLANES, t), F32)],
        compiler_params=_params("parallel", "parallel"),
        name="peer_route",
    )(q, keys)


def _gelu_tanh(x):
    return x * (0.5 * (1.0 + jnp.tanh(math.sqrt(2.0 / math.pi) * (x + 0.044715 * (x * x * x)))))


def _peer_kernel(x_ref, gf_ref, u_ref, v_ref, a_ref, b_ref, rows_ref, gfin_ref,
                 o_ref, h_ref, gate_ref, coef_ref):
    e = pl.program_id(1)
    te, tt = u_ref.shape[0], x_ref.shape[0]
    rows = te // PEER_NKEYS

    @pl.when(e == 0)
    def _():
        h_ref[...] = jnp.transpose(_rms(x_ref[...], gf_ref[...])).astype(BF16)
        o_ref[...] = jnp.zeros_like(o_ref)

    for il in range(rows):
        rs = slice(il * PEER_NKEYS, (il + 1) * PEER_NKEYS)
        a_rows = [a_ref[p, pl.ds(e * rows + il, 1), :] for p in range(PEER_HEADS)]
        for lb in range(tt // LANES):
            ls = slice(lb * LANES, (lb + 1) * LANES)
            g = None
            for p in range(PEER_HEADS):
                pair = a_rows[p][:, ls] + b_ref[p, :, ls]
                w = jnp.where(pair >= rows_ref[p, 0:1, ls], jnp.exp2(pair), 0.0)
                g = w if g is None else g + w
            gate_ref[rs, ls] = g
    act = _gelu_tanh(_dot(u_ref[...], h_ref[...]))
    coef_ref[...] = (act * gate_ref[...]).astype(BF16)
    o_ref[...] += _dot_tn(coef_ref[...], v_ref[...])

    @pl.when(e == pl.num_programs(1) - 1)
    def _():
        o_ref[...] = _rms(x_ref[...] + o_ref[...], gfin_ref[...])


def peer_experts(x1, gain_ffn, u, v, a, b, rows, gain_final, *, tt=1024, te=512):
    t, d = x1.shape
    tt, te = _tile(t, tt), _tile(PEER_EXPERTS, te)
    once = pl.Buffered(1)
    tab_spec = pl.BlockSpec((PEER_HEADS, PEER_NKEYS, tt), lambda i, e: (0, 0, i))
    return pl.pallas_call(
        _peer_kernel,
        grid=(t // tt, PEER_EXPERTS // te),
        in_specs=[pl.BlockSpec((tt, d), lambda i, e: (i, 0), pipeline_mode=once),
                  pl.BlockSpec((1, d), lambda i, e: (0, 0)),
                  pl.BlockSpec((te, d), lambda i, e: (e, 0)),
                  pl.BlockSpec((te, d), lambda i, e: (e, 0)),
                  tab_spec, tab_spec,
                  pl.BlockSpec((PEER_HEADS, SUBLANES, tt), lambda i, e: (0, 0, i)),
                  pl.BlockSpec((1, d), lambda i, e: (0, 0))],
        out_specs=pl.BlockSpec((tt, d), lambda i, e: (i, 0), pipeline_mode=once),
        out_shape=jax.ShapeDtypeStruct((t, d), F32),
        scratch_shapes=[pltpu.VMEM((d, tt), BF16),
                        pltpu.VMEM((te, tt), F32),
                        pltpu.VMEM((te, tt), BF16)],
        compiler_params=_params("parallel", "arbitrary", vmem_limit=PEER_VMEM_LIMIT_BYTES),
        name="peer_experts",
    )(x1, gain_ffn.reshape(1, d), u, v, a, b, rows, gain_final.reshape(1, d))


def _prepare_weights(w_in, gdn_a_log, gdn_dt_bias, w_branch_a, w_branch_b, w_out, peer_w_q,
                     peer_sub_keys, peer_u, peer_v):
    o_z = GDN_QKV + GDN_WIDTH
    o_ab = o_z + 4 * GDN_HEADS
    o_b = o_ab + 2 * DIFF_QK + DIFF_WIDTH
    w = w_in[0]
    pad = jnp.zeros((D_MODEL, LANES - 4 * GDN_HEADS), F32)
    row = lambda p: jnp.concatenate([p[0].reshape(1, -1), jnp.zeros((1, LANES - 2 * GDN_HEADS), F32)], axis=1)
    return dict(
        w_a=jnp.concatenate([w[:, :o_z], w[:, o_b:]], axis=1).astype(BF16),
        w_ab=jnp.concatenate([w[:, o_z:o_ab], pad], axis=1).astype(BF16),
        w_b=w[:, o_ab:o_b].astype(BF16),
        alog=row(gdn_a_log), dtb=row(gdn_dt_bias),
        qscale=jnp.concatenate([jnp.full((1, DIFF_QK), DIFF_HD ** -0.5 * LOG2E, F32),
                                jnp.ones((1, DIFF_QK + DIFF_WIDTH), F32)], axis=1),
        w_branch_a=w_branch_a[0].astype(BF16), w_branch_b=w_branch_b[0].astype(BF16),
        w_out=w_out[0].astype(BF16), w_q=peer_w_q[0].astype(BF16),
        keys=peer_sub_keys[0].astype(BF16), u=peer_u[0].astype(BF16), v=peer_v[0].astype(BF16),
    )


def _trunk(x, pw, norm_mix_gain, conv_w, gdn_norm_gain, diff_lambda, diff_norm_gain,
           norm_ffn_gain, norm_final_gain, slopes):
    b, s, d = x.shape
    t = b * s
    xt = x.reshape(t, d)
    g_mix = norm_mix_gain[0]

    proj_a = norm_matmul(xt, g_mix, pw["w_a"], F32, name="in_proj_a")
    proj_b = norm_matmul(xt, g_mix, pw["w_b"], BF16, epilogue=_scale_epilogue,
                         extra=(pw["qscale"],), group=2 * DIFF_HD, name="in_proj_b")
    gb = norm_matmul(xt, g_mix, pw["w_ab"], F32, epilogue=_gate_epilogue,
                     extra=(pw["alog"], pw["dtb"]), name="in_proj_gates")

    qkv_n = conv_prep(proj_a.reshape(b, s, -1), conv_w[0])
    g = gb[:, :2 * GDN_HEADS].reshape(b, s, 2, GDN_HEADS)
    beta = gb[:, 2 * GDN_HEADS:4 * GDN_HEADS].reshape(b, s, 2, GDN_HEADS)
    gb_col = jnp.transpose(jnp.concatenate([g, beta], axis=-1), (0, 2, 1, 3))
    gb_col = jnp.pad(gb_col, ((0, 0), (0, 0), (0, 0), (0, LANES - 2 * GDN_HEADS)))
    g_row = jnp.transpose(g, (0, 2, 3, 1))
    o_gdn = gdn_scan(qkv_n, gb_col, g_row).reshape(2, t, GDN_WIDTH)

    o_b = diff_attention(proj_b.reshape(-1, b, s, 2 * DIFF_HD), slopes, diff_lambda[0], diff_norm_gain[0])
    o_b = o_b.reshape(t, DIFF_WIDTH)

    merged = merge_branches(o_gdn, proj_a, o_b, pw["w_branch_a"], pw["w_branch_b"], gdn_norm_gain[0])
    x1 = matmul_residual(merged, pw["w_out"], xt, name="out_proj")

    q = norm_matmul(x1, norm_ffn_gain[0], pw["w_q"], BF16, name="peer_query")
    a, bb, rows = peer_route(q, pw["keys"])
    y = peer_experts(x1, norm_ffn_gain[0], pw["u"], pw["v"], a, bb, rows, norm_final_gain)
    return y.reshape(b, s, d)


def kernel(x_prompt, x_sample, norm_mix_gain, w_in, conv_w, gdn_a_log, gdn_dt_bias, gdn_norm_gain,
           diff_lambda, diff_norm_gain, w_branch_a, w_branch_b, w_out, norm_ffn_gain,
           peer_w_q, peer_sub_keys, peer_u, peer_v, norm_final_gain):
    pw = _prepare_weights(w_in, gdn_a_log, gdn_dt_bias, w_branch_a, w_branch_b, w_out, peer_w_q,
                          peer_sub_keys, peer_u, peer_v)
    slopes = 2.0 ** (-8.0 * jnp.arange(1, DIFF_HEADS + 1, dtype=F32) / DIFF_HEADS)
    run = functools.partial(_trunk, pw=pw, norm_mix_gain=norm_mix_gain, conv_w=conv_w,
                            gdn_norm_gain=gdn_norm_gain, diff_lambda=diff_lambda,
                            diff_norm_gain=diff_norm_gain, norm_ffn_gain=norm_ffn_gain,
                            norm_final_gain=norm_final_gain, slopes=slopes)
    return (run(x_prompt), run(x_sample))
```
